```python
import jax, jax.numpy as jnp
from jax import lax
import numpy as np

D_MODEL = 2048
BATCH = 2
SEQ = 8192
DEPTH = 1

CHUNK = 64
Q_BLOCK = 128
HEAD_DIM = 128
SB_HEADS = (D_MODEL // 2) // HEAD_DIM
SB_WIDTH = SB_HEADS * HEAD_DIM
HG_EXPAND = 128
HG_HEADS = (D_MODEL // 2) // HG_EXPAND
HG_WIDTH = HG_HEADS * HG_EXPAND
HG_VDIM = HG_WIDTH // HG_HEADS
MIX_WIDTH = SB_WIDTH + HG_WIDTH
IN_COLS = 3 * SB_WIDTH + 4 * HG_WIDTH
D_FF = 4 * D_MODEL
NORM_EPS = 1e-5

kernel_name = "hymba_stickbreak_hgrn2_block"


def _rmsnorm(x, w):
    xf = x.astype(jnp.float32)
    y = xf * lax.rsqrt(jnp.mean(xf * xf, axis=-1, keepdims=True) + NORM_EPS)
    return (y * w.astype(jnp.float32)).astype(x.dtype)


def _heads(t, n_heads):
    b, s, _ = t.shape
    return t.reshape(b, s, n_heads, -1).transpose(0, 2, 1, 3)


def _merge(t):
    b, n, s, d = t.shape
    return t.transpose(0, 2, 1, 3).reshape(b, s, n * d)


def _stick_breaking(q, k, v):
    seq = q.shape[2]
    scale = HEAD_DIM ** -0.5
    outs = []
    for blk in range(seq // Q_BLOCK):
        q0 = blk * Q_BLOCK
        q1 = q0 + Q_BLOCK
        qb = q[:, :, q0:q1].astype(jnp.float32)
        kp = k[:, :, :q1].astype(jnp.float32)
        vp = v[:, :, :q1]
        z = jnp.einsum('bhqd,bhkd->bhqk', qb, kp) * scale
        mask = jnp.arange(q1)[None, :] < jnp.arange(q0, q1)[:, None]
        log_1m = jnp.where(mask, jax.nn.log_sigmoid(-z), 0.0)
        log_after = lax.cumsum(log_1m, axis=3, reverse=True) - log_1m
        w = jnp.where(mask, jnp.exp(jax.nn.log_sigmoid(z) + log_after), 0.0)
        outs.append(jnp.einsum('bhqk,bhkd->bhqd', w.astype(vp.dtype), vp))
    return jnp.concatenate(outs, axis=2)


def _hgrn2(q, k, v, g):
    b_, h_, s_, dk = q.shape
    dv = v.shape[-1]
    n = s_ // CHUNK

    def to_chunks(t):
        return jnp.moveaxis(t.astype(jnp.float32).reshape(b_, h_, n, CHUNK, t.shape[-1]), 2, 0)

    causal = jnp.tril(jnp.ones((CHUNK, CHUNK), dtype=bool))

    def step(state, inp):
        qc, kc, vc, gc = inp
        cum = jnp.cumsum(gc, axis=2)
        diff = jnp.where(causal[:, :, None],
                         cum[:, :, :, None, :] - cum[:, :, None, :, :], -jnp.inf)
        scores = jnp.einsum('bhtk,bhtsk,bhsk->bhts', qc, jnp.exp(diff), kc)
        out = (jnp.einsum('bhts,bhsv->bhtv', scores, vc)
               + jnp.einsum('bhtk,bhkv->bhtv', qc * jnp.exp(cum), state))
        last = cum[:, :, -1:, :]
        new_state = (jnp.exp(last[:, :, 0, :])[..., None] * state
                     + jnp.einsum('bhsk,bhsv->bhkv', kc * jnp.exp(last - cum), vc))
        return new_state, out

    state0 = jnp.zeros((b_, h_, dk, dv), jnp.float32)
    _, outs = lax.scan(step, state0, (to_chunks(q), to_chunks(k), to_chunks(v), to_chunks(g)))
    return jnp.moveaxis(outs, 0, 2).reshape(b_, h_, s_, dv)


def setup_inputs(seed: int = 0) -> dict:
    key = jax.random.key(seed)
    ks = jax.random.split(key, 11)
    f32 = jnp.float32
    x = jax.random.normal(ks[0], (BATCH, SEQ, D_MODEL), f32)
    attn_norm_w = 1.0 + 0.01 * jax.random.normal(ks[1], (DEPTH, D_MODEL), f32)
    w_in = jax.random.normal(ks[2], (DEPTH, D_MODEL, IN_COLS), f32) * D_MODEL ** -0.5
    lb_logits = 0.1 * jax.random.normal(ks[3], (DEPTH + 1, HG_WIDTH), f32)
    sb_norm_w = 1.0 + 0.01 * jax.random.normal(ks[4], (DEPTH, HEAD_DIM), f32)
    hg_norm_w = 1.0 + 0.01 * jax.random.normal(ks[5], (DEPTH, HG_VDIM), f32)
    w_out = jax.random.normal(ks[6], (DEPTH, MIX_WIDTH, D_MODEL), f32) * MIX_WIDTH ** -0.5
    mlp_norm_w = 1.0 + 0.01 * jax.random.normal(ks[7], (DEPTH, D_MODEL), f32)
    w_up = jax.random.normal(ks[8], (DEPTH, D_MODEL, D_FF), f32) * D_MODEL ** -0.5
    w_down = jax.random.normal(ks[9], (DEPTH, D_FF, D_MODEL), f32) * D_FF ** -0.5
    final_norm_w = 1.0 + 0.01 * jax.random.normal(ks[10], (D_MODEL,), f32)
    return {"x": x, "attn_norm_w": attn_norm_w, "w_in": w_in, "lb_logits": lb_logits,
            "sb_norm_w": sb_norm_w, "hg_norm_w": hg_norm_w, "w_out": w_out,
            "mlp_norm_w": mlp_norm_w, "w_up": w_up, "w_down": w_down,
            "final_norm_w": final_norm_w}


def reference(x, attn_norm_w, w_in, lb_logits, sb_norm_w, hg_norm_w, w_out,
              mlp_norm_w, w_up, w_down, final_norm_w):
    splits = [SB_WIDTH, 2 * SB_WIDTH, 3 * SB_WIDTH, 3 * SB_WIDTH + HG_WIDTH,
              3 * SB_WIDTH + 2 * HG_WIDTH, 3 * SB_WIDTH + 3 * HG_WIDTH]
    lower_bounds = jnp.cumsum(jax.nn.softmax(lb_logits.astype(jnp.float32), axis=0), axis=0)
    h = x
    for layer in range(DEPTH):
        u = _rmsnorm(h, attn_norm_w[layer])
        proj = u @ w_in[layer]
        sb_q, sb_k, sb_v, hg_q, hg_f, hg_i, hg_g = jnp.split(proj, splits, axis=-1)

        o_a = _stick_breaking(_heads(sb_q, SB_HEADS), _heads(sb_k, SB_HEADS), _heads(sb_v, SB_HEADS))
        o_a = _merge(_rmsnorm(o_a, sb_norm_w[layer]))

        lb = lower_bounds[layer]
        f_logit = hg_f.astype(jnp.float32)
        log_f = jnp.log(lb + (1.0 - lb) * jax.nn.sigmoid(f_logit))
        k_in = (1.0 - lb) * jax.nn.sigmoid(-f_logit)
        o_b = _hgrn2(_heads(jax.nn.silu(hg_q), HG_HEADS), _heads(k_in, HG_HEADS),
                     _heads(hg_i, HG_HEADS), _heads(log_f, HG_HEADS)).astype(u.dtype)
        o_b = _merge(_rmsnorm(o_b, hg_norm_w[layer])) * jax.nn.silu(hg_g)

        h = h + jnp.concatenate([o_a, o_b], axis=-1) @ w_out[layer]

        m = _rmsnorm(h, mlp_norm_w[layer])
        h = h + jnp.square(jax.nn.relu(m @ w_up[layer])) @ w_down[layer]
    return _rmsnorm(h, final_norm_w)
```

```python
import functools

import jax
import jax.numpy as jnp
from jax import lax
from jax.experimental import pallas as pl
from jax.experimental.pallas import tpu as pltpu

F32 = jnp.float32
BF16 = jnp.bfloat16

HEAD_DIM = 128
NORM_EPS = 1e-5
VMEM_LIMIT_BYTES = 56 * 1024 * 1024

SB_ZERO_LOG = -105.0
HG_SAFE_LOG = 80.0

_NT = (((1,), (1,)), ((), ()))


def _rms(x, w):
    return x * lax.rsqrt(jnp.mean(x * x, axis=-1, keepdims=True) + NORM_EPS) * w


def _split_bf16(x):
    hi = x.astype(BF16)
    lo = (x - hi.astype(F32)).astype(BF16)
    return hi, lo


def _in_proj_kernel(x_ref, nw_ref, w_ref, o_ref, u_ref, *, q_blocks, q_scale):
    j = pl.program_id(1)

    @pl.when(j == 0)
    def _():
        u_ref[...] = _rms(x_ref[...], nw_ref[...]).astype(BF16)

    acc = jnp.dot(u_ref[...], w_ref[...], preferred_element_type=F32)
    scale = jnp.where(j < q_blocks, q_scale, 1.0).astype(F32)
    o_ref[...] = (acc * scale).astype(o_ref.dtype)


def _in_proj(x2, norm_w, w, *, sb_width, tm, tn):
    t, d = x2.shape
    n = w.shape[1]
    kern = functools.partial(_in_proj_kernel, q_blocks=sb_width // tn, q_scale=HEAD_DIM ** -0.5)
    return pl.pallas_call(
        kern,
        grid=(t // tm, n // tn),
        in_specs=[
            pl.BlockSpec((tm, d), lambda i, j: (i, 0)),
            pl.BlockSpec((1, d), lambda i, j: (0, 0)),
            pl.BlockSpec((d, tn), lambda i, j: (0, j)),
        ],
        out_specs=pl.BlockSpec((tm, tn), lambda i, j: (i, j)),
        out_shape=jax.ShapeDtypeStruct((t, n), BF16),
        scratch_shapes=[pltpu.VMEM((tm, d), BF16)],
        compiler_params=pltpu.CompilerParams(
            dimension_semantics=("parallel", "arbitrary"), vmem_limit_bytes=VMEM_LIMIT_BYTES),
        name="in_proj",
    )(x2, norm_w.reshape(1, d), w)


def _sb_kernel(q_ref, k_ref, v_ref, nw_ref, o_ref, acc_ref, carry_ref, *, heads, blk):
    i = pl.program_id(2)
    d = HEAD_DIM

    r = lax.broadcasted_iota(jnp.int32, (2 * blk, 2 * blk), 0) % blk
    c = lax.broadcasted_iota(jnp.int32, (2 * blk, 2 * blk), 1)
    wmat = jnp.where((c >= blk) | (r > c), -1.0, 0.0).astype(BF16)
    tri = (lax.broadcasted_iota(jnp.int32, (blk, blk), 1)
           < lax.broadcasted_iota(jnp.int32, (blk, blk), 0))

    def tile(h, k0, diag):
        hs = slice(h * d, (h + 1) * d)
        q = q_ref[0, :, hs]
        k = k_ref[0, pl.ds(k0, blk), hs]
        v = v_ref[0, pl.ds(k0, blk), hs]
        z = lax.dot_general(q, k, _NT, preferred_element_type=F32)
        sp = jnp.maximum(z, 0.0) + jnp.log(1.0 + jnp.exp(-jnp.abs(z)))
        log_beta = z - sp
        if diag:
            sp = jnp.where(tri, sp, 0.0)
        hi, lo = _split_bf16(sp)
        sums = jnp.dot(jnp.concatenate([hi, lo], axis=1), wmat, preferred_element_type=F32)
        carry = carry_ref[h]
        w = jnp.exp(log_beta + sums[:, :blk] + carry)
        if diag:
            w = jnp.where(tri, w, 0.0)
        pv = jnp.dot(w.astype(BF16), v, preferred_element_type=F32)
        if diag:
            acc_ref[h] = pv
            carry_ref[h] = sums[:, blk:]
        else:
            acc_ref[h] += pv
            carry_ref[h] = carry + sums[:, blk:]

    carry_ref[...] = jnp.zeros_like(carry_ref)
    k_diag = pl.multiple_of(i * blk, blk)
    for h in range(heads):
        tile(h, k_diag, True)

    def cond(state):
        j, live = state
        return jnp.logical_and(j >= 0, live > SB_ZERO_LOG)

    def body(state):
        j, _ = state
        k0 = pl.multiple_of(j * blk, blk)
        for h in range(heads):
            tile(h, k0, False)
        return j - 1, jnp.max(carry_ref[...])

    lax.while_loop(cond, body, (i - 1, jnp.max(carry_ref[...])))

    nw = nw_ref[...]
    for h in range(heads):
        o_ref[0, :, h * d:(h + 1) * d] = _rms(acc_ref[h], nw).astype(o_ref.dtype)


def _stick_break(proj3, norm_w, *, sb_width, heads_per_step, blk):
    b, s, _ = proj3.shape
    gw = heads_per_step * HEAD_DIM
    groups = sb_width // gw
    kern = functools.partial(_sb_kernel, heads=heads_per_step, blk=blk)
    return pl.pallas_call(
        kern,
        grid=(b, groups, s // blk),
        in_specs=[
            pl.BlockSpec((1, blk, gw), lambda bb, g, i: (bb, i, g)),
            pl.BlockSpec((1, s, gw), lambda bb, g, i: (bb, 0, groups + g)),
            pl.BlockSpec((1, s, gw), lambda bb, g, i: (bb, 0, 2 * groups + g)),
            pl.BlockSpec((1, HEAD_DIM), lambda bb, g, i: (0, 0)),
        ],
        out_specs=pl.BlockSpec((1, blk, gw), lambda bb, g, i: (bb, i, g)),
        out_shape=jax.ShapeDtypeStruct((b, s, sb_width), BF16),
        scratch_shapes=[pltpu.VMEM((heads_per_step, blk, HEAD_DIM), F32),
                        pltpu.VMEM((heads_per_step, blk, blk), F32)],
        compiler_params=pltpu.CompilerParams(
            dimension_semantics=("parallel", "parallel", "arbitrary"),
            vmem_limit_bytes=VMEM_LIMIT_BYTES),
        name="stick_break",
    )(proj3, proj3, proj3, norm_w.reshape(1, HEAD_DIM))


def _hgrn_kernel(q_ref, f_ref, i_ref, g_ref, lbl_ref, nw_ref, o_ref,
                 st_ref, cum_ref, *, heads, chunk, layer):
    d = HEAD_DIM
    n_chunks = q_ref.shape[1] // chunk
    tb = pl.program_id(1)

    @pl.when(tb == 0)
    def _():
        st_ref[...] = jnp.zeros_like(st_ref)

    lbl = lbl_ref[...]
    e = jnp.exp(lbl - jnp.max(lbl, axis=0, keepdims=True))
    lb_all = jnp.sum(e[:layer + 1], axis=0, keepdims=True) / jnp.sum(e, axis=0, keepdims=True)
    nw = nw_ref[...]

    rr = lax.broadcasted_iota(jnp.int32, (chunk, chunk), 0)
    cc = lax.broadcasted_iota(jnp.int32, (chunk, chunk), 1)
    incl = cc <= rr
    rr2 = lax.broadcasted_iota(jnp.int32, (chunk, 2 * chunk), 0)
    cc2 = lax.broadcasted_iota(jnp.int32, (chunk, 2 * chunk), 1) % chunk
    tril2 = jnp.where(cc2 <= rr2, 1.0, 0.0).astype(BF16)

    def gates(c0, h):
        hs = slice(h * d, (h + 1) * d)
        rows = pl.ds(c0, chunk)
        lb = lb_all[:, hs]
        fl = f_ref[0, rows, hs].astype(F32)
        ef = jnp.exp(-jnp.abs(fl))
        rf = 1.0 / (1.0 + ef)
        sig = jnp.where(fl >= 0, rf, ef * rf)
        nsig = jnp.where(fl >= 0, ef * rf, rf)
        log_f = jnp.log(lb + (1.0 - lb) * sig)
        k_in = (1.0 - lb) * nsig
        qr = q_ref[0, rows, hs].astype(F32)
        eq = jnp.exp(-jnp.abs(qr))
        rq = 1.0 / (1.0 + eq)
        qs = qr * jnp.where(qr >= 0, rq, eq * rq)
        return qs, k_in, log_f

    def finish(c0, h, out):
        hs = slice(h * d, (h + 1) * d)
        rows = pl.ds(c0, chunk)
        gr = g_ref[0, rows, hs].astype(F32)
        eg = jnp.exp(-jnp.abs(gr))
        rg = 1.0 / (1.0 + eg)
        gate = gr * jnp.where(gr >= 0, rg, eg * rg)
        o_ref[0, rows, hs] = (_rms(out, nw) * gate).astype(o_ref.dtype)

    def chunk_body(ci, _):
        c0 = pl.multiple_of(ci * chunk, chunk)
        old = ci % 2
        new = 1 - old
        worst = None
        for h in range(heads):
            hs = slice(h * d, (h + 1) * d)
            qs, k_in, log_f = gates(c0, h)
            hi, lo = _split_bf16(log_f)
            cum = jnp.dot(tril2, jnp.concatenate([hi, lo], axis=0),
                          preferred_element_type=F32)
            cum_ref[h] = cum
            last = cum[chunk - 1:chunk, :]
            qg = (qs * jnp.exp(cum)).astype(BF16)
            kg = (k_in * jnp.exp(jnp.minimum(-cum, HG_SAFE_LOG))).astype(BF16)
            kd = (k_in * jnp.exp(last - cum)).astype(BF16)
            v = i_ref[0, pl.ds(c0, chunk), hs]
            a = lax.dot_general(qg, kg, _NT, preferred_element_type=F32)
            a = jnp.where(incl, a, 0.0).astype(BF16)
            st_old = st_ref[old, h]
            out = (jnp.dot(a, v, preferred_element_type=F32)
                   + lax.dot_general(qg, st_old.astype(BF16), _NT, preferred_element_type=F32))
            finish(c0, h, out)
            vt = v.astype(F32).T.astype(BF16)
            st_ref[new, h] = st_old * jnp.exp(last) + jnp.dot(vt, kd, preferred_element_type=F32)
            lo_h = jnp.min(last)
            worst = lo_h if worst is None else jnp.minimum(worst, lo_h)

        @pl.when(worst < -HG_SAFE_LOG)
        def _():
            ones8 = jnp.ones((8, d), BF16)
            for h in range(heads):
                hs = slice(h * d, (h + 1) * d)
                qs, k_in, _unused = gates(c0, h)
                cum = cum_ref[h]
                v = i_ref[0, pl.ds(c0, chunk), hs]

                def row(t, _c):
                    sel = rr[:, :1] == t
                    q_t = jnp.sum(jnp.where(sel, qs, 0.0), axis=0, keepdims=True)
                    cum_t = jnp.sum(jnp.where(sel, cum, 0.0), axis=0, keepdims=True)
                    p = q_t * k_in * jnp.exp(jnp.minimum(cum_t - cum, 0.0))
                    srow = lax.dot_general(ones8, p.astype(BF16), _NT,
                                           preferred_element_type=F32)
                    a_ref_row = jnp.where(cc[:1, :] <= t, srow[:1, :], 0.0)
                    cum_ref[heads, pl.ds(t, 1), :] = a_ref_row
                    return _c

                lax.fori_loop(0, chunk, row, 0)
                a = cum_ref[heads].astype(BF16)
                qg = (qs * jnp.exp(cum)).astype(BF16)
                out = (jnp.dot(a, v, preferred_element_type=F32)
                       + lax.dot_general(qg, st_ref[old, h].astype(BF16), _NT,
                                         preferred_element_type=F32))
                finish(c0, h, out)

        return 0

    lax.fori_loop(0, n_chunks, chunk_body, 0)

    if n_chunks % 2 == 1:
        st_ref[0] = st_ref[1]


def _hgrn2(proj3, lb_logits, norm_w, *, col0, width, tb, chunk, layer):
    b, s, _ = proj3.shape
    heads = width // HEAD_DIM
    assert chunk == HEAD_DIM, "score scratch rows reuse the (chunk, HEAD_DIM) cum buffers"
    cb = col0 // width
    kern = functools.partial(_hgrn_kernel, heads=heads, chunk=chunk, layer=layer)

    def col(k):
        return pl.BlockSpec((1, tb, width), lambda bb, t: (bb, t, cb + k))

    return pl.pallas_call(
        kern,
        grid=(b, s // tb),
        in_specs=[col(0), col(1), col(2), col(3),
                  pl.BlockSpec(lb_logits.shape, lambda bb, t: (0, 0)),
                  pl.BlockSpec((1, HEAD_DIM), lambda bb, t: (0, 0))],
        out_specs=pl.BlockSpec((1, tb, width), lambda bb, t: (bb, t, 0)),
        out_shape=jax.ShapeDtypeStruct((b, s, width), BF16),
        scratch_shapes=[pltpu.VMEM((2, heads, HEAD_DIM, HEAD_DIM), F32),
                        pltpu.VMEM((heads + 1, chunk, HEAD_DIM), F32)],
        compiler_params=pltpu.CompilerParams(
            dimension_semantics=("parallel", "arbitrary"), vmem_limit_bytes=VMEM_LIMIT_BYTES),
        name="hgrn2",
    )(proj3, proj3, proj3, proj3, lb_logits, norm_w.reshape(1, HEAD_DIM))


def _out_proj_kernel(oa_ref, ob_ref, wa_ref, wb_ref, x_ref, nw_ref, h_ref, m_ref):
    acc = (jnp.dot(oa_ref[...], wa_ref[...], preferred_element_type=F32)
           + jnp.dot(ob_ref[...], wb_ref[...], preferred_element_type=F32))
    h = x_ref[...] + acc
    h_ref[...] = h
    m_ref[...] = _rms(h, nw_ref[...]).astype(m_ref.dtype)


def _out_proj(oa, ob, w_out, x2, norm_w, *, tm):
    t, d = x2.shape
    wa = oa.shape[1]
    wb = ob.shape[1]
    assert wa == wb
    return pl.pallas_call(
        _out_proj_kernel,
        grid=(t // tm,),
        in_specs=[
            pl.BlockSpec((tm, wa), lambda i: (i, 0)),
            pl.BlockSpec((tm, wb), lambda i: (i, 0)),
            pl.BlockSpec((wa, d), lambda i: (0, 0)),
            pl.BlockSpec((wb, d), lambda i: (1, 0)),
            pl.BlockSpec((tm, d), lambda i: (i, 0)),
            pl.BlockSpec((1, d), lambda i: (0, 0)),
        ],
        out_specs=[pl.BlockSpec((tm, d), lambda i: (i, 0)),
                   pl.BlockSpec((tm, d), lambda i: (i, 0))],
        out_shape=[jax.ShapeDtypeStruct((t, d), F32), jax.ShapeDtypeStruct((t, d), BF16)],
        compiler_params=pltpu.CompilerParams(
            dimension_semantics=("parallel",), vmem_limit_bytes=VMEM_LIMIT_BYTES),
        name="out_proj",
    )(oa, ob, w_out, w_out, x2, norm_w.reshape(1, d))


def _mlp_up_kernel(m_ref, w_ref, a_ref):
    acc = jnp.dot(m_ref[...], w_ref[...], preferred_element_type=F32)
    r = jnp.maximum(acc, 0.0)
    a_ref[...] = (r * r).astype(a_ref.dtype)


def _mlp_up(m, w_up, *, tm, tn):
    t, d = m.shape
    n = w_up.shape[1]
    return pl.pallas_call(
        _mlp_up_kernel,
        grid=(t // tm, n // tn),
        in_specs=[pl.BlockSpec((tm, d), lambda i, j: (i, 0)),
                  pl.BlockSpec((d, tn), lambda i, j: (0, j))],
        out_specs=pl.BlockSpec((tm, tn), lambda i, j: (i, j)),
        out_shape=jax.ShapeDtypeStruct((t, n), BF16),
        compiler_params=pltpu.CompilerParams(
            dimension_semantics=("parallel", "parallel"), vmem_limit_bytes=VMEM_LIMIT_BYTES),
        name="mlp_up",
    )(m, w_up)


def _mlp_down_kernel(a_ref, w_ref, h_ref, nw_ref, y_ref, acc_ref):
    j = pl.program_id(1)
    part = jnp.dot(a_ref[...], w_ref[...], preferred_element_type=F32)

    @pl.when(j == 0)
    def _():
        acc_ref[...] = h_ref[...] + part

    @pl.when(j > 0)
    def _():
        acc_ref[...] += part

    @pl.when(j == pl.num_programs(1) - 1)
    def _():
        y_ref[...] = _rms(acc_ref[...], nw_ref[...])


def _mlp_down(a, w_down, h, norm_w, *, tm, tk):
    t, f = a.shape
    d = w_down.shape[1]
    return pl.pallas_call(
        _mlp_down_kernel,
        grid=(t // tm, f // tk),
        in_specs=[pl.BlockSpec((tm, tk), lambda i, j: (i, j)),
                  pl.BlockSpec((tk, d), lambda i, j: (j, 0)),
                  pl.BlockSpec((tm, d), lambda i, j: (i, 0)),
                  pl.BlockSpec((1, d), lambda i, j: (0, 0))],
        out_specs=pl.BlockSpec((tm, d), lambda i, j: (i, 0)),
        out_shape=jax.ShapeDtypeStruct((t, d), F32),
        scratch_shapes=[pltpu.VMEM((tm, d), F32)],
        compiler_params=pltpu.CompilerParams(
            dimension_semantics=("parallel", "arbitrary"), vmem_limit_bytes=VMEM_LIMIT_BYTES),
        name="mlp_down",
    )(a, w_down, h, norm_w.reshape(1, d))


def _block(x, attn_norm_w, w_in, lb_logits, sb_norm_w, hg_norm_w, w_out,
           mlp_norm_w, w_up, w_down, final_norm_w, *, tiles):
    b, s, d = x.shape
    depth = w_in.shape[0]
    sb_width = d // 2
    hg_width = d // 2
    x2 = x.reshape(b * s, d)
    h = x2
    for layer in range(depth):
        proj = _in_proj(h, attn_norm_w[layer], w_in[layer].astype(BF16),
                        sb_width=sb_width, tm=tiles["proj_tm"], tn=tiles["proj_tn"])
        proj3 = proj.reshape(b, s, -1)
        o_a = _stick_break(proj3, sb_norm_w[layer], sb_width=sb_width,
                           heads_per_step=tiles["sb_heads"], blk=tiles["sb_blk"])
        o_b = _hgrn2(proj3, lb_logits, hg_norm_w[layer], col0=3 * sb_width, width=hg_width,
                     tb=tiles["hg_tb"], chunk=HEAD_DIM, layer=layer)
        h, m = _out_proj(o_a.reshape(b * s, sb_width), o_b.reshape(b * s, hg_width),
                         w_out[layer].astype(BF16), h, mlp_norm_w[layer], tm=tiles["out_tm"])
        a = _mlp_up(m, w_up[layer].astype(BF16), tm=tiles["up_tm"], tn=tiles["up_tn"])
        assert layer == depth - 1, "only the single-layer block is fused end to end"
        h = _mlp_down(a, w_down[layer].astype(BF16), h, final_norm_w,
                      tm=tiles["down_tm"], tk=tiles["down_tk"])
    return h.reshape(b, s, d)


_TILES = dict(proj_tm=512, proj_tn=1024, sb_heads=4, sb_blk=128, hg_tb=1024,
              out_tm=512, up_tm=1024, up_tn=1024, down_tm=512, down_tk=1024)


def kernel(x, attn_norm_w, w_in, lb_logits, sb_norm_w, hg_norm_w, w_out,
           mlp_norm_w, w_up, w_down, final_norm_w):
    return _block(x, attn_norm_w, w_in, lb_logits, sb_norm_w, hg_norm_w, w_out,
                  mlp_norm_w, w_up, w_down, final_norm_w, tiles=_TILES)
```

```python
import functools

import jax
import jax.numpy as jnp
from jax import lax
from jax.experimental import pallas as pl
from jax.experimental.pallas import tpu as pltpu

F32 = jnp.float32
BF16 = jnp.bfloat16

HEAD_DIM = 128
NORM_EPS = 1e-5
VMEM_LIMIT_BYTES = 56 * 1024 * 1024

LANES = 128
LOG2_E = 1.4426950408889634
SB_ZERO_LOG2 = -152.0
HG_SAFE_LOG = 80.0

_NT = (((1,), (1,)), ((), ()))


def _rms(x, w):
    return x * lax.rsqrt(jnp.mean(x * x, axis=-1, keepdims=True) + NORM_EPS) * w


def _split_bf16(x):
    hi = x.astype(BF16)
    lo = (x - hi.astype(F32)).astype(BF16)
    return hi, lo


def _in_proj_kernel(x_ref, nw_ref, w_ref, o_ref, u_ref, *, q_blocks, q_scale):
    j = pl.program_id(1)

    @pl.when(j == 0)
    def _():
        u_ref[...] = _rms(x_ref[...], nw_ref[...]).astype(BF16)

    acc = jnp.dot(u_ref[...], w_ref[...], preferred_element_type=F32)
    scale = jnp.where(j < q_blocks, q_scale, 1.0).astype(F32)
    o_ref[...] = (acc * scale).astype(o_ref.dtype)


def _in_proj(x2, norm_w, w, *, sb_width, tm, tn):
    t, d = x2.shape
    n = w.shape[1]
    kern = functools.partial(_in_proj_kernel, q_blocks=sb_width // tn,
                             q_scale=HEAD_DIM ** -0.5 * LOG2_E)
    return pl.pallas_call(
        kern,
        grid=(t // tm, n // tn),
        in_specs=[
            pl.BlockSpec((tm, d), lambda i, j: (i, 0)),
            pl.BlockSpec((1, d), lambda i, j: (0, 0)),
            pl.BlockSpec((d, tn), lambda i, j: (0, j)),
        ],
        out_specs=pl.BlockSpec((tm, tn), lambda i, j: (i, j)),
        out_shape=jax.ShapeDtypeStruct((t, n), BF16),
        scratch_shapes=[pltpu.VMEM((tm, d), BF16)],
        compiler_params=pltpu.CompilerParams(
            dimension_semantics=("parallel", "arbitrary"), vmem_limit_bytes=VMEM_LIMIT_BYTES),
        name="in_proj",
    )(x2, norm_w.reshape(1, d), w)


def _sb_kernel(q_ref, k_ref, v_ref, nw_ref, o_ref, acc_ref, carry_ref, *, heads, blk):
    i = pl.program_id(2)
    d = HEAD_DIM
    sub = LANES
    nsub = blk // sub

    r = lax.broadcasted_iota(jnp.int32, (2 * sub, 2 * sub), 0) % sub
    c = lax.broadcasted_iota(jnp.int32, (2 * sub, 2 * sub), 1)
    wmat = jnp.where((c >= sub) | (r > c), -1.0, 0.0).astype(BF16)
    tri = (lax.broadcasted_iota(jnp.int32, (blk, blk), 1)
           < lax.broadcasted_iota(jnp.int32, (blk, blk), 0))

    hsl = [slice(h * d, (h + 1) * d) for h in range(heads)]
    tri_all = jnp.concatenate([tri] * heads, axis=0)

    def tiles(k0, diag):
        rows = pl.ds(k0, blk)
        z = jnp.concatenate(
            [lax.dot_general(q_ref[0, :, hs], k_ref[0, rows, hs], _NT, preferred_element_type=F32)
             for hs in hsl], axis=0)
        sp = jnp.maximum(z, 0.0) + jnp.log2(1.0 + jnp.exp2(-jnp.abs(z)))
        log_beta = z - sp
        if diag:
            sp = jnp.where(tri_all, sp, 0.0)
        hi, lo = _split_bf16(sp)
        carry = None if diag else carry_ref[...]
        after = [None] * nsub
        for cb in reversed(range(nsub)):
            cs = slice(cb * sub, (cb + 1) * sub)
            sums = jnp.dot(jnp.concatenate([hi[:, cs], lo[:, cs]], axis=1), wmat,
                           preferred_element_type=F32)
            after[cb] = sums[:, :sub] if carry is None else sums[:, :sub] + carry
            carry = sums[:, sub:] if carry is None else carry + sums[:, sub:]
        carry_ref[...] = carry
        w = jnp.exp2(log_beta + jnp.concatenate(after, axis=1))
        if diag:
            w = jnp.where(tri_all, w, 0.0)
        wb = w.astype(BF16)
        for h, hs in enumerate(hsl):
            pv = jnp.dot(wb[h * blk:(h + 1) * blk], v_ref[0, rows, hs],
                         preferred_element_type=F32)
            if diag:
                acc_ref[h] = pv
            else:
                acc_ref[h] += pv

    tiles(pl.multiple_of(i * blk, blk), True)

    def cond(state):
        j, live = state
        return jnp.logical_and(j >= 0, live > SB_ZERO_LOG2)

    def body(state):
        j, _ = state
        tiles(pl.multiple_of(j * blk, blk), False)
        return j - 1, jnp.max(carry_ref[...])

    lax.while_loop(cond, body, (i - 1, jnp.max(carry_ref[...])))

    nw = nw_ref[...]
    for h in range(heads):
        o_ref[0, :, h * d:(h + 1) * d] = _rms(acc_ref[h], nw).astype(o_ref.dtype)


def _stick_break(proj3, norm_w, *, sb_width, heads_per_step, blk):
    b, s, _ = proj3.shape
    gw = heads_per_step * HEAD_DIM
    groups = sb_width // gw
    kern = functools.partial(_sb_kernel, heads=heads_per_step, blk=blk)
    return pl.pallas_call(
        kern,
        grid=(b, groups, s // blk),
        in_specs=[
            pl.BlockSpec((1, blk, gw), lambda bb, g, i: (bb, i, g)),
            pl.BlockSpec((1, s, gw), lambda bb, g, i: (bb, 0, groups + g)),
            pl.BlockSpec((1, s, gw), lambda bb, g, i: (bb, 0, 2 * groups + g)),
            pl.BlockSpec((1, HEAD_DIM), lambda bb, g, i: (0, 0)),
        ],
        out_specs=pl.BlockSpec((1, blk, gw), lambda bb, g, i: (bb, i, g)),
        out_shape=jax.ShapeDtypeStruct((b, s, sb_width), BF16),
        scratch_shapes=[pltpu.VMEM((heads_per_step, blk, HEAD_DIM), F32),
                        pltpu.VMEM((heads_per_step * blk, LANES), F32)],
        compiler_params=pltpu.CompilerParams(
            dimension_semantics=("parallel", "parallel", "arbitrary"),
            vmem_limit_bytes=VMEM_LIMIT_BYTES),
        name="stick_break",
    )(proj3, proj3, proj3, norm_w.reshape(1, HEAD_DIM))


def _hgrn_kernel(q_ref, f_ref, i_ref, g_ref, lbl_ref, nw_ref, o_ref,
                 st_ref, cum_ref, *, heads, chunk, layer):
    d = HEAD_DIM
    n_chunks = q_ref.shape[1] // chunk
    tb = pl.program_id(1)

    @pl.when(tb == 0)
    def _():
        st_ref[...] = jnp.zeros_like(st_ref)

    lbl = lbl_ref[...]
    e = jnp.exp(lbl - jnp.max(lbl, axis=0, keepdims=True))
    lb_all = jnp.sum(e[:layer + 1], axis=0, keepdims=True) / jnp.sum(e, axis=0, keepdims=True)
    nw = nw_ref[...]

    rr = lax.broadcasted_iota(jnp.int32, (chunk, chunk), 0)
    cc = lax.broadcasted_iota(jnp.int32, (chunk, chunk), 1)
    incl = cc <= rr
    rr2 = lax.broadcasted_iota(jnp.int32, (chunk, 2 * chunk), 0)
    cc2 = lax.broadcasted_iota(jnp.int32, (chunk, 2 * chunk), 1) % chunk
    tril2 = jnp.where(cc2 <= rr2, 1.0, 0.0).astype(BF16)

    def gates(c0, h):
        hs = slice(h * d, (h + 1) * d)
        rows = pl.ds(c0, chunk)
        lb = lb_all[:, hs]
        t1 = (0.5 * (1.0 - lb)) * jnp.tanh(0.5 * f_ref[0, rows, hs].astype(F32))
        log_f = jnp.log(0.5 * (1.0 + lb) + t1)
        k_in = 0.5 * (1.0 - lb) - t1
        hq = 0.5 * q_ref[0, rows, hs].astype(F32)
        qs = hq + hq * jnp.tanh(hq)
        return qs, k_in, log_f

    def finish(c0, h, out):
        hs = slice(h * d, (h + 1) * d)
        rows = pl.ds(c0, chunk)
        hg = 0.5 * g_ref[0, rows, hs].astype(F32)
        gate = hg + hg * jnp.tanh(hg)
        o_ref[0, rows, hs] = (_rms(out, nw) * gate).astype(o_ref.dtype)

    def chunk_body(ci, _):
        c0 = pl.multiple_of(ci * chunk, chunk)
        old = ci % 2
        new = 1 - old
        hr = range(heads)
        hsl = [slice(h * d, (h + 1) * d) for h in hr]
        g = [gates(c0, h) for h in hr]
        cum = []
        for h in hr:
            hi, lo = _split_bf16(g[h][2])
            cum.append(jnp.dot(tril2, jnp.concatenate([hi, lo], axis=0),
                               preferred_element_type=F32))
            cum_ref[h] = cum[h]
        last = [cum[h][chunk - 1:chunk, :] for h in hr]
        qg = [(g[h][0] * jnp.exp(cum[h])).astype(BF16) for h in hr]
        kd = [g[h][1] * jnp.exp(last[h] - cum[h]) for h in hr]
        kg = [(kd[h] * jnp.exp(jnp.minimum(-last[h], HG_SAFE_LOG))).astype(BF16) for h in hr]
        v = [i_ref[0, pl.ds(c0, chunk), hs] for hs in hsl]
        a = [lax.dot_general(qg[h], kg[h], _NT, preferred_element_type=F32) for h in hr]
        st_old = [st_ref[old, h] for h in hr]
        inter = [lax.dot_general(qg[h], st_old[h].astype(BF16), _NT, preferred_element_type=F32)
                 for h in hr]
        for h in hr:
            vt = v[h].astype(F32).T.astype(BF16)
            st_ref[new, h] = (st_old[h] * jnp.exp(last[h])
                              + jnp.dot(vt, kd[h].astype(BF16), preferred_element_type=F32))
        for h in hr:
            am = jnp.where(incl, a[h], 0.0).astype(BF16)
            finish(c0, h, jnp.dot(am, v[h], preferred_element_type=F32) + inter[h])
        worst = functools.reduce(jnp.minimum, [jnp.min(last[h]) for h in hr])

        @pl.when(worst < -HG_SAFE_LOG)
        def _():
            ones8 = jnp.ones((8, d), BF16)
            for h in range(heads):
                hs = slice(h * d, (h + 1) * d)
                qs, k_in, _unused = gates(c0, h)
                cum = cum_ref[h]
                v = i_ref[0, pl.ds(c0, chunk), hs]

                def row(t, _c):
                    sel = rr[:, :1] == t
                    q_t = jnp.sum(jnp.where(sel, qs, 0.0), axis=0, keepdims=True)
                    cum_t = jnp.sum(jnp.where(sel, cum, 0.0), axis=0, keepdims=True)
                    p = q_t * k_in * jnp.exp(jnp.minimum(cum_t - cum, 0.0))
                    srow = lax.dot_general(ones8, p.astype(BF16), _NT,
                                           preferred_element_type=F32)
                    a_ref_row = jnp.where(cc[:1, :] <= t, srow[:1, :], 0.0)
                    cum_ref[heads, pl.ds(t, 1), :] = a_ref_row
                    return _c

                lax.fori_loop(0, chunk, row, 0)
                a = cum_ref[heads].astype(BF16)
                qg = (qs * jnp.exp(cum)).astype(BF16)
                out = (jnp.dot(a, v, preferred_element_type=F32)
                       + lax.dot_general(qg, st_ref[old, h].astype(BF16), _NT,
                                         preferred_element_type=F32))
                finish(c0, h, out)

        return 0

    lax.fori_loop(0, n_chunks, chunk_body, 0)

    if n_chunks % 2 == 1:
        st_ref[0] = st_ref[1]


def _hgrn2(proj3, lb_logits, norm_w, *, col0, width, tb, chunk, layer):
    b, s, _ = proj3.shape
    heads = width // HEAD_DIM
    assert chunk == HEAD_DIM, "score scratch rows reuse the (chunk, HEAD_DIM) cum buffers"
    cb = col0 // width
    kern = functools.partial(_hgrn_kernel, heads=heads, chunk=chunk, layer=layer)

    def col(k):
        return pl.BlockSpec((1, tb, width), lambda bb, t: (bb, t, cb + k))

    return pl.pallas_call(
        kern,
        grid=(b, s // tb),
        in_specs=[col(0), col(1), col(2), col(3),
                  pl.BlockSpec(lb_logits.shape, lambda bb, t: (0, 0)),
                  pl.BlockSpec((1, HEAD_DIM), lambda bb, t: (0, 0))],
        out_specs=pl.BlockSpec((1, tb, width), lambda bb, t: (bb, t, 0)),
        out_shape=jax.ShapeDtypeStruct((b, s, width), BF16),
        scratch_shapes=[pltpu.VMEM((2, heads, HEAD_DIM, HEAD_DIM), F32),
                        pltpu.VMEM((heads + 1, chunk, HEAD_DIM), F32)],
        compiler_params=pltpu.CompilerParams(
            dimension_semantics=("parallel", "arbitrary"), vmem_limit_bytes=VMEM_LIMIT_BYTES),
        name="hgrn2",
    )(proj3, proj3, proj3, proj3, lb_logits, norm_w.reshape(1, HEAD_DIM))


def _out_proj_kernel(oa_ref, ob_ref, wa_ref, wb_ref, x_ref, nw_ref, h_ref, m_ref):
    acc = (jnp.dot(oa_ref[...], wa_ref[...], preferred_element_type=F32)
           + jnp.dot(ob_ref[...], wb_ref[...], preferred_element_type=F32))
    h = x_ref[...] + acc
    h_ref[...] = h
    m_ref[...] = _rms(h, nw_ref[...]).astype(m_ref.dtype)


def _out_proj(oa, ob, w_out, x2, norm_w, *, tm):
    t, d = x2.shape
    wa = oa.shape[1]
    wb = ob.shape[1]
    assert wa == wb
    return pl.pallas_call(
        _out_proj_kernel,
        grid=(t // tm,),
        in_specs=[
            pl.BlockSpec((tm, wa), lambda i: (i, 0)),
            pl.BlockSpec((tm, wb), lambda i: (i, 0)),
            pl.BlockSpec((wa, d), lambda i: (0, 0)),
            pl.BlockSpec((wb, d), lambda i: (1, 0)),
            pl.BlockSpec((tm, d), lambda i: (i, 0)),
            pl.BlockSpec((1, d), lambda i: (0, 0)),
        ],
        out_specs=[pl.BlockSpec((tm, d), lambda i: (i, 0)),
                   pl.BlockSpec((tm, d), lambda i: (i, 0))],
        out_shape=[jax.ShapeDtypeStruct((t, d), F32), jax.ShapeDtypeStruct((t, d), BF16)],
        compiler_params=pltpu.CompilerParams(
            dimension_semantics=("parallel",), vmem_limit_bytes=VMEM_LIMIT_BYTES),
        name="out_proj",
    )(oa, ob, w_out, w_out, x2, norm_w.reshape(1, d))


def _mlp_kernel(m_ref, wu_ref, wd_ref, h_ref, nw_ref, y_ref, acc_ref):
    j = pl.program_id(1)

    @pl.when(j == 0)
    def _():
        acc_ref[...] = h_ref[...]

    r = jnp.maximum(jnp.dot(m_ref[...], wu_ref[...], preferred_element_type=F32), 0.0)
    acc_ref[...] += jnp.dot((r * r).astype(BF16), wd_ref[...], preferred_element_type=F32)

    @pl.when(j == pl.num_programs(1) - 1)
    def _():
        y_ref[...] = _rms(acc_ref[...], nw_ref[...])


def _mlp(m, w_up, w_down, h, norm_w, *, tm, tf):
    t, d = m.shape
    f = w_up.shape[1]
    return pl.pallas_call(
        _mlp_kernel,
        grid=(t // tm, f // tf),
        in_specs=[pl.BlockSpec((tm, d), lambda i, j: (i, 0)),
                  pl.BlockSpec((d, tf), lambda i, j: (0, j)),
                  pl.BlockSpec((tf, d), lambda i, j: (j, 0)),
                  pl.BlockSpec((tm, d), lambda i, j: (i, 0)),
                  pl.BlockSpec((1, d), lambda i, j: (0, 0))],
        out_specs=pl.BlockSpec((tm, d), lambda i, j: (i, 0)),
        out_shape=jax.ShapeDtypeStruct((t, d), F32),
        scratch_shapes=[pltpu.VMEM((tm, d), F32)],
        compiler_params=pltpu.CompilerParams(
            dimension_semantics=("parallel", "arbitrary"), vmem_limit_bytes=VMEM_LIMIT_BYTES),
        name="mlp",
    )(m, w_up, w_down, h, norm_w.reshape(1, d))


def _block(x, attn_norm_w, w_in, lb_logits, sb_norm_w, hg_norm_w, w_out,
           mlp_norm_w, w_up, w_down, final_norm_w, *, tiles):
    b, s, d = x.shape
    depth = w_in.shape[0]
    sb_width = d // 2
    hg_width = d // 2
    x2 = x.reshape(b * s, d)
    h = x2
    for layer in range(depth):
        proj = _in_proj(h, attn_norm_w[layer], w_in[layer].astype(BF16),
                        sb_width=sb_width, tm=tiles["proj_tm"], tn=tiles["proj_tn"])
        proj3 = proj.reshape(b, s, -1)
        o_a = _stick_break(proj3, sb_norm_w[layer], sb_width=sb_width,
                           heads_per_step=tiles["sb_heads"], blk=tiles["sb_blk"])
        o_b = _hgrn2(proj3, lb_logits, hg_norm_w[layer], col0=3 * sb_width, width=hg_width,
                     tb=tiles["hg_tb"], chunk=HEAD_DIM, layer=layer)
        h, m = _out_proj(o_a.reshape(b * s, sb_width), o_b.reshape(b * s, hg_width),
                         w_out[layer].astype(BF16), h, mlp_norm_w[layer], tm=tiles["out_tm"])
        assert layer == depth - 1, "only the single-layer block is fused end to end"
        h = _mlp(m, w_up[layer].astype(BF16), w_down[layer].astype(BF16), h, final_norm_w,
                 tm=tiles["mlp_tm"], tf=tiles["mlp_tf"])
    return h.reshape(b, s, d)


_TILES = dict(proj_tm=1024, proj_tn=1024, sb_heads=4, sb_blk=256, hg_tb=1024,
              out_tm=512, mlp_tm=512, mlp_tf=1024)


def kernel(x, attn_norm_w, w_in, lb_logits, sb_norm_w, hg_norm_w, w_out,
           mlp_norm_w, w_up, w_down, final_norm_w):
    return _block(x, attn_norm_w, w_in, lb_logits, sb_norm_w, hg_norm_w, w_out,
                  mlp_norm_w, w_up, w_down, final_norm_w, tiles=_TILES)
```

```python
import functools

import jax
import jax.numpy as jnp
from jax import lax
from jax.experimental import pallas as pl
from jax.experimental.pallas import tpu as pltpu

F32 = jnp.float32
BF16 = jnp.bfloat16

HEAD_DIM = 128
NORM_EPS = 1e-5
VMEM_LIMIT_BYTES = 56 * 1024 * 1024

LANES = 128
LOG2_E = 1.4426950408889634
SB_ZERO_LOG2 = -152.0
SB_NO_BLOCK_LOG2 = -1e30
HG_SAFE_LOG = 80.0

_NT = (((1,), (1,)), ((), ()))


def _rms(x, w):
    return x * lax.rsqrt(jnp.mean(x * x, axis=-1, keepdims=True) + NORM_EPS) * w


def _split_bf16(x):
    hi = x.astype(BF16)
    lo = (x - hi.astype(F32)).astype(BF16)
    return hi, lo


def _in_proj_kernel(x_ref, nw_ref, w_ref, o_ref, u_ref, *, q_blocks, q_scale):
    j = pl.program_id(1)

    @pl.when(j == 0)
    def _():
        u_ref[...] = _rms(x_ref[...], nw_ref[...]).astype(BF16)

    acc = jnp.dot(u_ref[...], w_ref[...].astype(BF16), preferred_element_type=F32)
    scale = jnp.where(j < q_blocks, q_scale, 1.0).astype(F32)
    o_ref[...] = (acc * scale).astype(o_ref.dtype)


def _in_proj(x2, norm_w, w, *, sb_width, tm, tn):
    t, d = x2.shape
    n = w.shape[1]
    kern = functools.partial(_in_proj_kernel, q_blocks=sb_width // tn,
                             q_scale=HEAD_DIM ** -0.5 * LOG2_E)
    return pl.pallas_call(
        kern,
        grid=(t // tm, n // tn),
        in_specs=[
            pl.BlockSpec((tm, d), lambda i, j: (i, 0)),
            pl.BlockSpec((1, d), lambda i, j: (0, 0)),
            pl.BlockSpec((d, tn), lambda i, j: (0, j)),
        ],
        out_specs=pl.BlockSpec((tm, tn), lambda i, j: (i, j)),
        out_shape=jax.ShapeDtypeStruct((t, n), BF16),
        scratch_shapes=[pltpu.VMEM((tm, d), BF16)],
        compiler_params=pltpu.CompilerParams(
            dimension_semantics=("parallel", "arbitrary"), vmem_limit_bytes=VMEM_LIMIT_BYTES),
        name="in_proj",
    )(x2, norm_w.reshape(1, d), w)


def _sb_kernel(q_ref, kd_ref, vd_ref, kp_ref, vp_ref, proj_hbm, nw_ref, wu_ref, wd_ref,
               o_ref, wub_ref, wdb_ref,
               acc_ref, carry_ref, kbuf, vbuf, sem, *, heads, blk, k_col, v_col):
    b = pl.program_id(0)
    i = pl.program_id(1)
    d = HEAD_DIM
    sub = LANES
    nsub = blk // sub
    width = heads * d

    wub_ref[...] = wu_ref[...].astype(BF16)
    wdb_ref[...] = wd_ref[...].astype(BF16)

    r = lax.broadcasted_iota(jnp.int32, (2 * sub, 2 * sub), 0) % sub
    c = lax.broadcasted_iota(jnp.int32, (2 * sub, 2 * sub), 1)
    wmat = jnp.where((c >= sub) | (r > c), -1.0, 0.0).astype(BF16)
    tri = (lax.broadcasted_iota(jnp.int32, (blk, blk), 1)
           < lax.broadcasted_iota(jnp.int32, (blk, blk), 0))

    hsl = [slice(h * d, (h + 1) * d) for h in range(heads)]
    tri_all = jnp.concatenate([tri] * heads, axis=0)

    def diag_only(x, fill):
        return jnp.concatenate([x[:, :-blk], jnp.where(tri_all, x[:, -blk:], fill)], axis=1)

    def tiles(srcs, diag, older_bias=None):
        nb = len(srcs)
        z = jnp.concatenate(
            [jnp.concatenate(
                [lax.dot_general(q_ref[0, :, hs], k_src[:, hs], _NT, preferred_element_type=F32)
                 for k_src, _ in srcs], axis=1)
             for hs in hsl], axis=0)
        neg_abs = lax.bitcast_convert_type(
            lax.bitcast_convert_type(z, jnp.uint32) | jnp.uint32(0x80000000), F32)
        sp = jnp.maximum(z, 0.0) + jnp.log2(1.0 + jnp.exp2(neg_abs))
        log_beta = z - sp
        if diag:
            sp = diag_only(sp, 0.0)
        hi, lo = _split_bf16(sp)
        carry = None if diag else carry_ref[...]
        after = [None] * (nb * nsub)
        for cb in reversed(range(nb * nsub)):
            cs = slice(cb * sub, (cb + 1) * sub)
            sums = jnp.dot(jnp.concatenate([hi[:, cs], lo[:, cs]], axis=1), wmat,
                           preferred_element_type=F32)
            after[cb] = sums[:, :sub] if carry is None else sums[:, :sub] + carry
            carry = sums[:, sub:] if carry is None else carry + sums[:, sub:]
            if diag and older_bias is not None and cb == (nb - 1) * nsub:
                carry = carry + older_bias
        carry_ref[...] = carry
        w = jnp.exp2(log_beta + jnp.concatenate(after, axis=1))
        if diag:
            w = diag_only(w, 0.0)
        wb = w.astype(BF16)
        for h, hs in enumerate(hsl):
            pv = None
            for n, (_, v_src) in enumerate(srcs):
                part = jnp.dot(wb[h * blk:(h + 1) * blk, n * blk:(n + 1) * blk], v_src[:, hs],
                               preferred_element_type=F32)
                pv = part if pv is None else pv + part
            if diag:
                acc_ref[h] = pv
            else:
                acc_ref[h] += pv

    no_prev = jnp.where(i >= 1, 0.0, SB_NO_BLOCK_LOG2).astype(F32)
    tiles([(kp_ref.at[0], vp_ref.at[0]), (kd_ref.at[0], vd_ref.at[0])], True, no_prev)

    def fetch(j):
        rows = pl.ds(pl.multiple_of(j * blk, blk), blk)
        return (pltpu.make_async_copy(proj_hbm.at[b, rows, pl.ds(k_col, width)], kbuf, sem.at[0]),
                pltpu.make_async_copy(proj_hbm.at[b, rows, pl.ds(v_col, width)], vbuf, sem.at[1]))

    def cond(state):
        j, live = state
        return jnp.logical_and(j >= 0, live > SB_ZERO_LOG2)

    def body(state):
        j, _ = state
        for cp in fetch(j):
            cp.start()
        for cp in fetch(j):
            cp.wait()
        tiles([(kbuf, vbuf)], False)
        return j - 1, jnp.max(carry_ref[...])

    lax.while_loop(cond, body, (i - 2, jnp.max(carry_ref[...])))

    nw = nw_ref[...]
    for h in range(heads):
        o_ref[0, :, h * d:(h + 1) * d] = _rms(acc_ref[h], nw).astype(o_ref.dtype)


def _stick_break(proj3, norm_w, w_up, w_down, *, sb_width, blk):
    b, s, _ = proj3.shape
    heads = sb_width // HEAD_DIM
    nq = s // blk
    steps = b * nq
    d_model, d_ff = w_up.shape
    ru, rd = d_model // steps, d_ff // steps
    assert ru * steps == d_model and rd * steps == d_ff
    kern = functools.partial(_sb_kernel, heads=heads, blk=blk, k_col=sb_width, v_col=2 * sb_width)
    return pl.pallas_call(
        kern,
        grid=(b, nq),
        in_specs=[
            pl.BlockSpec((1, blk, sb_width), lambda bb, i: (bb, i, 0)),
            pl.BlockSpec((1, blk, sb_width), lambda bb, i: (bb, i, 1)),
            pl.BlockSpec((1, blk, sb_width), lambda bb, i: (bb, i, 2)),
            pl.BlockSpec((1, blk, sb_width), lambda bb, i: (bb, jnp.maximum(i - 1, 0), 1)),
            pl.BlockSpec((1, blk, sb_width), lambda bb, i: (bb, jnp.maximum(i - 1, 0), 2)),
            pl.BlockSpec(memory_space=pl.ANY),
            pl.BlockSpec((1, HEAD_DIM), lambda bb, i: (0, 0)),
            pl.BlockSpec((ru, d_ff), lambda bb, i: (bb * nq + i, 0)),
            pl.BlockSpec((rd, d_model), lambda bb, i: (bb * nq + i, 0)),
        ],
        out_specs=[pl.BlockSpec((1, blk, sb_width), lambda bb, i: (bb, i, 0)),
                   pl.BlockSpec((ru, d_ff), lambda bb, i: (bb * nq + i, 0)),
                   pl.BlockSpec((rd, d_model), lambda bb, i: (bb * nq + i, 0))],
        out_shape=[jax.ShapeDtypeStruct((b, s, sb_width), BF16),
                   jax.ShapeDtypeStruct(w_up.shape, BF16),
                   jax.ShapeDtypeStruct(w_down.shape, BF16)],
        scratch_shapes=[pltpu.VMEM((heads, blk, HEAD_DIM), F32),
                        pltpu.VMEM((heads * blk, LANES), F32),
                        pltpu.VMEM((blk, sb_width), BF16),
                        pltpu.VMEM((blk, sb_width), BF16),
                        pltpu.SemaphoreType.DMA((2,))],
        compiler_params=pltpu.CompilerParams(
            dimension_semantics=("parallel", "arbitrary"), vmem_limit_bytes=VMEM_LIMIT_BYTES),
        name="stick_break",
    )(proj3, proj3, proj3, proj3, proj3, proj3, norm_w.reshape(1, HEAD_DIM), w_up, w_down)


def _hgrn_kernel(q_ref, f_ref, i_ref, g_ref, lbl_ref, nw_ref, o_ref,
                 st_ref, cum_ref, *, heads, chunk, layer):
    d = HEAD_DIM
    n_chunks = q_ref.shape[1] // chunk
    tb = pl.program_id(1)

    @pl.when(tb == 0)
    def _():
        st_ref[...] = jnp.zeros_like(st_ref)

    lbl = lbl_ref[...]
    e = jnp.exp(lbl - jnp.max(lbl, axis=0, keepdims=True))
    lb_all = jnp.sum(e[:layer + 1], axis=0, keepdims=True) / jnp.sum(e, axis=0, keepdims=True)
    nw = nw_ref[...]

    rr = lax.broadcasted_iota(jnp.int32, (chunk, chunk), 0)
    cc = lax.broadcasted_iota(jnp.int32, (chunk, chunk), 1)
    incl = cc <= rr
    rr2 = lax.broadcasted_iota(jnp.int32, (chunk, 2 * chunk), 0)
    cc2 = lax.broadcasted_iota(jnp.int32, (chunk, 2 * chunk), 1) % chunk
    tril2 = jnp.where(cc2 <= rr2, 1.0, 0.0).astype(BF16)

    def gates(c0, h):
        hs = slice(h * d, (h + 1) * d)
        rows = pl.ds(c0, chunk)
        lb = lb_all[:, hs]
        t1 = (0.5 * (1.0 - lb)) * jnp.tanh(0.5 * f_ref[0, rows, hs].astype(F32))
        log_f = jnp.log(0.5 * (1.0 + lb) + t1)
        k_in = 0.5 * (1.0 - lb) - t1
        hq = 0.5 * q_ref[0, rows, hs].astype(F32)
        qs = hq + hq * jnp.tanh(hq)
        return qs, k_in, log_f

    def finish(c0, h, out):
        hs = slice(h * d, (h + 1) * d)
        rows = pl.ds(c0, chunk)
        hg = 0.5 * g_ref[0, rows, hs].astype(F32)
        gate = hg + hg * jnp.tanh(hg)
        o_ref[0, rows, hs] = (_rms(out, nw) * gate).astype(o_ref.dtype)

    def chunk_body(ci, _):
        c0 = pl.multiple_of(ci * chunk, chunk)
        old = ci % 2
        new = 1 - old
        hr = range(heads)
        hsl = [slice(h * d, (h + 1) * d) for h in hr]
        g = [gates(c0, h) for h in hr]
        cum = []
        for h in hr:
            hi, lo = _split_bf16(g[h][2])
            cum.append(jnp.dot(tril2, jnp.concatenate([hi, lo], axis=0),
                               preferred_element_type=F32))
            cum_ref[h] = cum[h]
        last = [cum[h][chunk - 1:chunk, :] for h in hr]
        qg = [(g[h][0] * jnp.exp(cum[h])).astype(BF16) for h in hr]
        kd = [g[h][1] * jnp.exp(last[h] - cum[h]) for h in hr]
        kg = [(kd[h] * jnp.exp(jnp.minimum(-last[h], HG_SAFE_LOG))).astype(BF16) for h in hr]
        v = [i_ref[0, pl.ds(c0, chunk), hs] for hs in hsl]
        a = [lax.dot_general(qg[h], kg[h], _NT, preferred_element_type=F32) for h in hr]
        st_old = [st_ref[old, h] for h in hr]
        inter = [lax.dot_general(qg[h], st_old[h].astype(BF16), _NT, preferred_element_type=F32)
                 for h in hr]
        for h in hr:
            vt = v[h].astype(F32).T.astype(BF16)
            st_ref[new, h] = (st_old[h] * jnp.exp(last[h])
                              + jnp.dot(vt, kd[h].astype(BF16), preferred_element_type=F32))
        for h in hr:
            am = jnp.where(incl, a[h], 0.0).astype(BF16)
            finish(c0, h, jnp.dot(am, v[h], preferred_element_type=F32) + inter[h])
        worst = functools.reduce(jnp.minimum, [jnp.min(last[h]) for h in hr])

        @pl.when(worst < -HG_SAFE_LOG)
        def _():
            ones8 = jnp.ones((8, d), BF16)
            for h in range(heads):
                hs = slice(h * d, (h + 1) * d)
                qs, k_in, _unused = gates(c0, h)
                cum = cum_ref[h]
                v = i_ref[0, pl.ds(c0, chunk), hs]

                def row(t, _c):
                    sel = rr[:, :1] == t
                    q_t = jnp.sum(jnp.where(sel, qs, 0.0), axis=0, keepdims=True)
                    cum_t = jnp.sum(jnp.where(sel, cum, 0.0), axis=0, keepdims=True)
                    p = q_t * k_in * jnp.exp(jnp.minimum(cum_t - cum, 0.0))
                    srow = lax.dot_general(ones8, p.astype(BF16), _NT,
                                           preferred_element_type=F32)
                    a_ref_row = jnp.where(cc[:1, :] <= t, srow[:1, :], 0.0)
                    cum_ref[heads, pl.ds(t, 1), :] = a_ref_row
                    return _c

                lax.fori_loop(0, chunk, row, 0)
                a = cum_ref[heads].astype(BF16)
                qg = (qs * jnp.exp(cum)).astype(BF16)
                out = (jnp.dot(a, v, preferred_element_type=F32)
                       + lax.dot_general(qg, st_ref[old, h].astype(BF16), _NT,
                                         preferred_element_type=F32))
                finish(c0, h, out)

        return 0

    lax.fori_loop(0, n_chunks, chunk_body, 0)

    if n_chunks % 2 == 1:
        st_ref[0] = st_ref[1]


def _hgrn2(proj3, lb_logits, norm_w, *, col0, width, tb, chunk, layer):
    b, s, _ = proj3.shape
    heads = width // HEAD_DIM
    assert chunk == HEAD_DIM, "score scratch rows reuse the (chunk, HEAD_DIM) cum buffers"
    cb = col0 // width
    kern = functools.partial(_hgrn_kernel, heads=heads, chunk=chunk, layer=layer)

    def col(k):
        return pl.BlockSpec((1, tb, width), lambda bb, t: (bb, t, cb + k))

    return pl.pallas_call(
        kern,
        grid=(b, s // tb),
        in_specs=[col(0), col(1), col(2), col(3),
                  pl.BlockSpec(lb_logits.shape, lambda bb, t: (0, 0)),
                  pl.BlockSpec((1, HEAD_DIM), lambda bb, t: (0, 0))],
        out_specs=pl.BlockSpec((1, tb, width), lambda bb, t: (bb, t, 0)),
        out_shape=jax.ShapeDtypeStruct((b, s, width), BF16),
        scratch_shapes=[pltpu.VMEM((2, heads, HEAD_DIM, HEAD_DIM), F32),
                        pltpu.VMEM((heads + 1, chunk, HEAD_DIM), F32)],
        compiler_params=pltpu.CompilerParams(
            dimension_semantics=("parallel", "arbitrary"), vmem_limit_bytes=VMEM_LIMIT_BYTES),
        name="hgrn2",
    )(proj3, proj3, proj3, proj3, lb_logits, norm_w.reshape(1, HEAD_DIM))


def _out_proj_kernel(oa_ref, ob_ref, wa_ref, wb_ref, x_ref, nw_ref, h_ref, m_ref):
    acc = (jnp.dot(oa_ref[...], wa_ref[...], preferred_element_type=F32)
           + jnp.dot(ob_ref[...], wb_ref[...], preferred_element_type=F32))
    h = x_ref[...] + acc
    h_ref[...] = h
    m_ref[...] = _rms(h, nw_ref[...]).astype(m_ref.dtype)


def _out_proj(oa, ob, w_out, x2, norm_w, *, tm):
    t, d = x2.shape
    wa = oa.shape[1]
    wb = ob.shape[1]
    assert wa == wb
    return pl.pallas_call(
        _out_proj_kernel,
        grid=(t // tm,),
        in_specs=[
            pl.BlockSpec((tm, wa), lambda i: (i, 0)),
            pl.BlockSpec((tm, wb), lambda i: (i, 0)),
            pl.BlockSpec((wa, d), lambda i: (0, 0)),
            pl.BlockSpec((wb, d), lambda i: (1, 0)),
            pl.BlockSpec((tm, d), lambda i: (i, 0)),
            pl.BlockSpec((1, d), lambda i: (0, 0)),
        ],
        out_specs=[pl.BlockSpec((tm, d), lambda i: (i, 0)),
                   pl.BlockSpec((tm, d), lambda i: (i, 0))],
        out_shape=[jax.ShapeDtypeStruct((t, d), F32), jax.ShapeDtypeStruct((t, d), BF16)],
        compiler_params=pltpu.CompilerParams(
            dimension_semantics=("parallel",), vmem_limit_bytes=VMEM_LIMIT_BYTES),
        name="out_proj",
    )(oa, ob, w_out, w_out, x2, norm_w.reshape(1, d))


def _mlp_kernel(m_ref, wu_ref, wd_ref, h_ref, nw_ref, y_ref, acc_ref):
    j = pl.program_id(1)

    @pl.when(j == 0)
    def _():
        acc_ref[...] = h_ref[...]

    r = jnp.maximum(jnp.dot(m_ref[...], wu_ref[...], preferred_element_type=F32), 0.0)
    acc_ref[...] += jnp.dot((r * r).astype(BF16), wd_ref[...], preferred_element_type=F32)

    @pl.when(j == pl.num_programs(1) - 1)
    def _():
        y_ref[...] = _rms(acc_ref[...], nw_ref[...])


def _mlp(m, w_up, w_down, h, norm_w, *, tm, tf):
    t, d = m.shape
    f = w_up.shape[1]
    return pl.pallas_call(
        _mlp_kernel,
        grid=(t // tm, f // tf),
        in_specs=[pl.BlockSpec((tm, d), lambda i, j: (i, 0)),
                  pl.BlockSpec((d, tf), lambda i, j: (0, j)),
                  pl.BlockSpec((tf, d), lambda i, j: (j, 0)),
                  pl.BlockSpec((tm, d), lambda i, j: (i, 0)),
                  pl.BlockSpec((1, d), lambda i, j: (0, 0))],
        out_specs=pl.BlockSpec((tm, d), lambda i, j: (i, 0)),
        out_shape=jax.ShapeDtypeStruct((t, d), F32),
        scratch_shapes=[pltpu.VMEM((tm, d), F32)],
        compiler_params=pltpu.CompilerParams(
            dimension_semantics=("parallel", "arbitrary"), vmem_limit_bytes=VMEM_LIMIT_BYTES),
        name="mlp",
    )(m, w_up, w_down, h, norm_w.reshape(1, d))


def _block(x, attn_norm_w, w_in, lb_logits, sb_norm_w, hg_norm_w, w_out,
           mlp_norm_w, w_up, w_down, final_norm_w, *, tiles):
    b, s, d = x.shape
    depth = w_in.shape[0]
    sb_width = d // 2
    hg_width = d // 2
    x2 = x.reshape(b * s, d)
    h = x2
    for layer in range(depth):
        proj = _in_proj(h, attn_norm_w[layer], w_in[layer],
                        sb_width=sb_width, tm=tiles["proj_tm"], tn=tiles["proj_tn"])
        proj3 = proj.reshape(b, s, -1)
        o_a, w_up_bf, w_down_bf = _stick_break(proj3, sb_norm_w[layer], w_up[layer], w_down[layer],
                                               sb_width=sb_width, blk=tiles["sb_blk"])
        o_b = _hgrn2(proj3, lb_logits, hg_norm_w[layer], col0=3 * sb_width, width=hg_width,
                     tb=tiles["hg_tb"], chunk=HEAD_DIM, layer=layer)
        h, m = _out_proj(o_a.reshape(b * s, sb_width), o_b.reshape(b * s, hg_width),
                         w_out[layer].astype(BF16), h, mlp_norm_w[layer], tm=tiles["out_tm"])
        assert layer == depth - 1, "only the single-layer block is fused end to end"
        h = _mlp(m, w_up_bf, w_down_bf, h, final_norm_w,
                 tm=tiles["mlp_tm"], tf=tiles["mlp_tf"])
    return h.reshape(b, s, d)


_TILES = dict(proj_tm=1024, proj_tn=1024, sb_blk=256, hg_tb=1024,
              out_tm=512, mlp_tm=512, mlp_tf=1024)


def kernel(x, attn_norm_w, w_in, lb_logits, sb_norm_w, hg_norm_w, w_out,
           mlp_norm_w, w_up, w_down, final_norm_w):
    return _block(x, attn_norm_w, w_in, lb_logits, sb_norm_w, hg_norm_w, w_out,
                  mlp_norm_w, w_up, w_down, final_norm_w, tiles=_TILES)
```

```python
import functools

import jax
import jax.numpy as jnp
from jax import lax
from jax.experimental import pallas as pl
from jax.experimental.pallas import tpu as pltpu

F32 = jnp.float32
BF16 = jnp.bfloat16

HEAD_DIM = 128
NORM_EPS = 1e-5
VMEM_LIMIT_BYTES = 56 * 1024 * 1024

LANES = 128
LOG2_E = 1.4426950408889634
SB_ZERO_LOG2 = -152.0
SB_NO_BLOCK_LOG2 = -1e30
HG_SAFE_LOG = 80.0
HG_CHUNKS_PER_ITER = 2

_NT = (((1,), (1,)), ((), ()))


def _rms(x, w):
    return x * lax.rsqrt(jnp.mean(x * x, axis=-1, keepdims=True) + NORM_EPS) * w


def _split_bf16(x):
    hi = x.astype(BF16)
    lo = (x - hi.astype(F32)).astype(BF16)
    return hi, lo


def _in_proj_kernel(x_ref, nw_ref, w_ref, o_ref, u_ref, *, col_scales):
    j = pl.program_id(1)

    @pl.when(j == 0)
    def _():
        u_ref[...] = _rms(x_ref[...], nw_ref[...]).astype(BF16)

    acc = jnp.dot(u_ref[...], w_ref[...].astype(BF16), preferred_element_type=F32)
    scale = jnp.float32(col_scales[0])
    for jj in range(1, len(col_scales)):
        scale = jnp.where(j == jj, jnp.float32(col_scales[jj]), scale)
    o_ref[...] = (acc * scale).astype(o_ref.dtype)


def _in_proj(x2, norm_w, w, *, group_scales, tm, tn):
    t, d = x2.shape
    n = w.shape[1]
    per_group = n // len(group_scales) // tn
    assert per_group * tn * len(group_scales) == n
    kern = functools.partial(
        _in_proj_kernel, col_scales=tuple(s for s in group_scales for _ in range(per_group)))
    return pl.pallas_call(
        kern,
        grid=(t // tm, n // tn),
        in_specs=[
            pl.BlockSpec((tm, d), lambda i, j: (i, 0)),
            pl.BlockSpec((1, d), lambda i, j: (0, 0)),
            pl.BlockSpec((d, tn), lambda i, j: (0, j)),
        ],
        out_specs=pl.BlockSpec((tm, tn), lambda i, j: (i, j)),
        out_shape=jax.ShapeDtypeStruct((t, n), BF16),
        scratch_shapes=[pltpu.VMEM((tm, d), BF16)],
        compiler_params=pltpu.CompilerParams(
            dimension_semantics=("parallel", "arbitrary"), vmem_limit_bytes=VMEM_LIMIT_BYTES),
        name="in_proj",
    )(x2, norm_w.reshape(1, d), w)


def _sb_kernel(*refs, heads, blk, nprev, k_col, v_col):
    q_ref = refs[0]
    k_refs = refs[1:2 + nprev]
    v_refs = refs[2 + nprev:3 + 2 * nprev]
    (proj_hbm, nw_ref, wu_ref, wd_ref, o_ref, wub_ref, wdb_ref,
     acc_ref, carry_ref, kbuf, vbuf, sem) = refs[3 + 2 * nprev:]
    b = pl.program_id(0)
    i = pl.program_id(1)
    d = HEAD_DIM
    sub = LANES
    nsub = blk // sub
    width = heads * d

    wub_ref[...] = wu_ref[...].astype(BF16)
    wdb_ref[...] = wd_ref[...].astype(BF16)

    r = lax.broadcasted_iota(jnp.int32, (2 * sub, 2 * sub), 0) % sub
    c = lax.broadcasted_iota(jnp.int32, (2 * sub, 2 * sub), 1)
    wmat = jnp.where((c >= sub) | (r > c), -1.0, 0.0).astype(BF16)
    tri = (lax.broadcasted_iota(jnp.int32, (blk, blk), 1)
           < lax.broadcasted_iota(jnp.int32, (blk, blk), 0))

    hsl = [slice(h * d, (h + 1) * d) for h in range(heads)]
    tri_all = jnp.concatenate([tri] * heads, axis=0)

    def diag_only(x, fill):
        return jnp.concatenate([x[:, :-blk], jnp.where(tri_all, x[:, -blk:], fill)], axis=1)

    def tiles(srcs, diag, older_bias=()):
        nb = len(srcs)
        z = jnp.concatenate(
            [jnp.concatenate(
                [lax.dot_general(q_ref[0, :, hs], k_src[:, hs], _NT, preferred_element_type=F32)
                 for k_src, _ in srcs], axis=1)
             for hs in hsl], axis=0)
        sp = jnp.maximum(z, 0.0) + jnp.log2(1.0 + jnp.exp2(-jnp.abs(z)))
        log_beta = z - sp
        if diag:
            sp = diag_only(sp, 0.0)
        hi, lo = _split_bf16(sp)
        carry = None if diag else carry_ref[...]
        after = [None] * (nb * nsub)
        for cb in reversed(range(nb * nsub)):
            cs = slice(cb * sub, (cb + 1) * sub)
            sums = jnp.dot(jnp.concatenate([hi[:, cs], lo[:, cs]], axis=1), wmat,
                           preferred_element_type=F32)
            after[cb] = sums[:, :sub] if carry is None else sums[:, :sub] + carry
            carry = sums[:, sub:] if carry is None else carry + sums[:, sub:]
            if diag and cb % nsub == 0 and 0 < cb // nsub < nb:
                carry = carry + older_bias[nb - 1 - cb // nsub]
        carry_ref[...] = carry
        w = jnp.exp2(log_beta + jnp.concatenate(after, axis=1))
        if diag:
            w = diag_only(w, 0.0)
        wb = w.astype(BF16)
        for h, hs in enumerate(hsl):
            pv = None
            for n, (_, v_src) in enumerate(srcs):
                part = jnp.dot(wb[h * blk:(h + 1) * blk, n * blk:(n + 1) * blk], v_src[:, hs],
                               preferred_element_type=F32)
                pv = part if pv is None else pv + part
            if diag:
                acc_ref[h] = pv
            else:
                acc_ref[h] += pv

    missing = [jnp.where(i >= k, 0.0, SB_NO_BLOCK_LOG2).astype(F32) for k in range(1, nprev + 1)]
    tiles([(k_refs[k].at[0], v_refs[k].at[0]) for k in reversed(range(nprev + 1))], True, missing)

    def fetch(j):
        rows = pl.ds(pl.multiple_of(j * blk, blk), blk)
        return (pltpu.make_async_copy(proj_hbm.at[b, rows, pl.ds(k_col, width)], kbuf, sem.at[0]),
                pltpu.make_async_copy(proj_hbm.at[b, rows, pl.ds(v_col, width)], vbuf, sem.at[1]))

    def cond(state):
        j, live = state
        return jnp.logical_and(j >= 0, live > SB_ZERO_LOG2)

    def body(state):
        j, _ = state
        for cp in fetch(j):
            cp.start()
        for cp in fetch(j):
            cp.wait()
        tiles([(kbuf, vbuf)], False)
        return j - 1, jnp.max(carry_ref[...])

    lax.while_loop(cond, body, (i - 1 - nprev, jnp.max(carry_ref[...])))

    nw = nw_ref[...]
    for h in range(heads):
        o_ref[0, :, h * d:(h + 1) * d] = _rms(acc_ref[h], nw).astype(o_ref.dtype)


def _stick_break(proj3, norm_w, w_up, w_down, *, sb_width, blk, nprev):
    b, s, _ = proj3.shape
    heads = sb_width // HEAD_DIM
    nq = s // blk
    steps = b * nq
    d_model, d_ff = w_up.shape
    ru, rd = d_model // steps, d_ff // steps
    assert ru * steps == d_model and rd * steps == d_ff
    kern = functools.partial(_sb_kernel, heads=heads, blk=blk, nprev=nprev,
                             k_col=sb_width, v_col=2 * sb_width)

    def back(k, col):
        return pl.BlockSpec((1, blk, sb_width), lambda bb, i: (bb, jnp.maximum(i - k, 0), col))

    return pl.pallas_call(
        kern,
        grid=(b, nq),
        in_specs=[
            back(0, 0),
            *[back(k, 1) for k in range(nprev + 1)],
            *[back(k, 2) for k in range(nprev + 1)],
            pl.BlockSpec(memory_space=pl.ANY),
            pl.BlockSpec((1, HEAD_DIM), lambda bb, i: (0, 0)),
            pl.BlockSpec((ru, d_ff), lambda bb, i: (bb * nq + i, 0)),
            pl.BlockSpec((rd, d_model), lambda bb, i: (bb * nq + i, 0)),
        ],
        out_specs=[pl.BlockSpec((1, blk, sb_width), lambda bb, i: (bb, i, 0)),
                   pl.BlockSpec((ru, d_ff), lambda bb, i: (bb * nq + i, 0)),
                   pl.BlockSpec((rd, d_model), lambda bb, i: (bb * nq + i, 0))],
        out_shape=[jax.ShapeDtypeStruct((b, s, sb_width), BF16),
                   jax.ShapeDtypeStruct(w_up.shape, BF16),
                   jax.ShapeDtypeStruct(w_down.shape, BF16)],
        scratch_shapes=[pltpu.VMEM((heads, blk, HEAD_DIM), F32),
                        pltpu.VMEM((heads * blk, LANES), F32),
                        pltpu.VMEM((blk, sb_width), BF16),
                        pltpu.VMEM((blk, sb_width), BF16),
                        pltpu.SemaphoreType.DMA((2,))],
        compiler_params=pltpu.CompilerParams(
            dimension_semantics=("parallel", "arbitrary"), vmem_limit_bytes=VMEM_LIMIT_BYTES),
        name="stick_break",
    )(*([proj3] * (4 + 2 * nprev)), norm_w.reshape(1, HEAD_DIM), w_up, w_down)


def _hgrn_kernel(q_ref, f_ref, i_ref, g_ref, lbl_ref, nw_ref, o_ref,
                 st_ref, cum_ref, *, heads, chunk, layer):
    d = HEAD_DIM
    n_chunks = q_ref.shape[1] // chunk
    cpi = HG_CHUNKS_PER_ITER
    assert n_chunks % cpi == 0
    tb = pl.program_id(1)

    @pl.when(tb == 0)
    def _():
        st_ref[...] = jnp.zeros_like(st_ref)

    lbl = lbl_ref[...]
    e = jnp.exp(lbl - jnp.max(lbl, axis=0, keepdims=True))
    lb_all = jnp.sum(e[:layer + 1], axis=0, keepdims=True) / jnp.sum(e, axis=0, keepdims=True)
    nw = nw_ref[...]

    rr = lax.broadcasted_iota(jnp.int32, (chunk, chunk), 0)
    cc = lax.broadcasted_iota(jnp.int32, (chunk, chunk), 1)
    incl = cc <= rr
    rr2 = lax.broadcasted_iota(jnp.int32, (chunk, 2 * chunk), 0)
    cc2 = lax.broadcasted_iota(jnp.int32, (chunk, 2 * chunk), 1) % chunk
    tril2 = jnp.where(cc2 <= rr2, 1.0, 0.0).astype(BF16)

    def gates(c0, h):
        hs = slice(h * d, (h + 1) * d)
        rows = pl.ds(c0, chunk)
        lb = lb_all[:, hs]
        t1 = (0.5 * (1.0 - lb)) * jnp.tanh(f_ref[0, rows, hs].astype(F32))
        log_f = jnp.log(0.5 * (1.0 + lb) + t1)
        k_in = 0.5 * (1.0 - lb) - t1
        hq = q_ref[0, rows, hs].astype(F32)
        qs = hq + hq * jnp.tanh(hq)
        return qs, k_in, log_f

    def finish(c0, h, out):
        hs = slice(h * d, (h + 1) * d)
        rows = pl.ds(c0, chunk)
        hg = g_ref[0, rows, hs].astype(F32)
        gate = hg + hg * jnp.tanh(hg)
        o_ref[0, rows, hs] = (_rms(out, nw) * gate).astype(o_ref.dtype)

    def fixup(c0, u):
        ones8 = jnp.ones((8, d), BF16)
        scores = cum_ref.at[cpi * heads]
        for h in range(heads):
            hs = slice(h * d, (h + 1) * d)
            qs, k_in, _unused = gates(c0, h)
            cum = cum_ref[u * heads + h]
            v = i_ref[0, pl.ds(c0, chunk), hs]

            def row(t, _c):
                sel = rr[:, :1] == t
                q_t = jnp.sum(jnp.where(sel, qs, 0.0), axis=0, keepdims=True)
                cum_t = jnp.sum(jnp.where(sel, cum, 0.0), axis=0, keepdims=True)
                p = q_t * k_in * jnp.exp(jnp.minimum(cum_t - cum, 0.0))
                srow = lax.dot_general(ones8, p.astype(BF16), _NT,
                                       preferred_element_type=F32)
                scores[pl.ds(t, 1), :] = jnp.where(cc[:1, :] <= t, srow[:1, :], 0.0)
                return _c

            lax.fori_loop(0, chunk, row, 0)
            qg = (qs * jnp.exp(cum)).astype(BF16)
            out = (jnp.dot(scores[...].astype(BF16), v, preferred_element_type=F32)
                   + lax.dot_general(qg, st_ref[u, h].astype(BF16), _NT,
                                     preferred_element_type=F32))
            finish(c0, h, out)

    def iter_body(it, _):
        hr = range(heads)
        hsl = [slice(h * d, (h + 1) * d) for h in hr]
        c0 = [pl.multiple_of((it * cpi + u) * chunk, chunk) for u in range(cpi)]
        uh = [(u, h) for u in range(cpi) for h in hr]
        g = {k: gates(c0[k[0]], k[1]) for k in uh}
        cum = {}
        for u, h in uh:
            hi, lo = _split_bf16(g[u, h][2])
            cum[u, h] = jnp.dot(tril2, jnp.concatenate([hi, lo], axis=0),
                                preferred_element_type=F32)
            cum_ref[u * heads + h] = cum[u, h]
        last = {k: cum[k][chunk - 1:chunk, :] for k in uh}
        qg = {k: (g[k][0] * jnp.exp(cum[k])).astype(BF16) for k in uh}
        kd = {k: g[k][1] * jnp.exp(last[k] - cum[k]) for k in uh}
        kg = {k: (kd[k] * jnp.exp(jnp.minimum(-last[k], HG_SAFE_LOG))).astype(BF16) for k in uh}
        v = {(u, h): i_ref[0, pl.ds(c0[u], chunk), hsl[h]] for u, h in uh}
        a = {k: lax.dot_general(qg[k], kg[k], _NT, preferred_element_type=F32) for k in uh}
        st = [st_ref[0, h] for h in hr]
        inter = {}
        for u in range(cpi):
            for h in hr:
                inter[u, h] = lax.dot_general(qg[u, h], st[h].astype(BF16), _NT,
                                              preferred_element_type=F32)
            for h in hr:
                vt = v[u, h].astype(F32).T.astype(BF16)
                st[h] = (st[h] * jnp.exp(last[u, h])
                         + jnp.dot(vt, kd[u, h].astype(BF16), preferred_element_type=F32))
                st_ref[u + 1, h] = st[h]
        for u, h in uh:
            am = jnp.where(incl, a[u, h], 0.0).astype(BF16)
            finish(c0[u], h, jnp.dot(am, v[u, h], preferred_element_type=F32) + inter[u, h])
        worst = [functools.reduce(jnp.minimum, [jnp.min(last[u, h]) for h in hr])
                 for u in range(cpi)]

        for u in range(cpi):
            pl.when(worst[u] < -HG_SAFE_LOG)(functools.partial(fixup, c0[u], u))

        st_ref[0] = st_ref[cpi]
        return 0

    lax.fori_loop(0, n_chunks // cpi, iter_body, 0)


def _hgrn2(proj3, lb_logits, norm_w, *, col0, width, tb, chunk, layer):
    b, s, _ = proj3.shape
    heads = width // HEAD_DIM
    assert chunk == HEAD_DIM, "score scratch rows reuse the (chunk, HEAD_DIM) cum buffers"
    cb = col0 // width
    kern = functools.partial(_hgrn_kernel, heads=heads, chunk=chunk, layer=layer)

    def col(k):
        return pl.BlockSpec((1, tb, width), lambda bb, t: (bb, t, cb + k))

    return pl.pallas_call(
        kern,
        grid=(b, s // tb),
        in_specs=[col(0), col(1), col(2), col(3),
                  pl.BlockSpec(lb_logits.shape, lambda bb, t: (0, 0)),
                  pl.BlockSpec((1, HEAD_DIM), lambda bb, t: (0, 0))],
        out_specs=pl.BlockSpec((1, tb, width), lambda bb, t: (bb, t, 0)),
        out_shape=jax.ShapeDtypeStruct((b, s, width), BF16),
        scratch_shapes=[pltpu.VMEM((HG_CHUNKS_PER_ITER + 1, heads, HEAD_DIM, HEAD_DIM), F32),
                        pltpu.VMEM((HG_CHUNKS_PER_ITER * heads + 1, chunk, HEAD_DIM), F32)],
        compiler_params=pltpu.CompilerParams(
            dimension_semantics=("parallel", "arbitrary"), vmem_limit_bytes=VMEM_LIMIT_BYTES),
        name="hgrn2",
    )(proj3, proj3, proj3, proj3, lb_logits, norm_w.reshape(1, HEAD_DIM))


def _out_proj_kernel(oa_ref, ob_ref, wa_ref, wb_ref, x_ref, nw_ref, h_ref, m_ref):
    acc = (jnp.dot(oa_ref[...], wa_ref[...], preferred_element_type=F32)
           + jnp.dot(ob_ref[...], wb_ref[...], preferred_element_type=F32))
    h = x_ref[...] + acc
    h_ref[...] = h
    m_ref[...] = _rms(h, nw_ref[...]).astype(m_ref.dtype)


def _out_proj(oa, ob, w_out, x2, norm_w, *, tm):
    t, d = x2.shape
    wa = oa.shape[1]
    wb = ob.shape[1]
    assert wa == wb
    return pl.pallas_call(
        _out_proj_kernel,
        grid=(t // tm,),
        in_specs=[
            pl.BlockSpec((tm, wa), lambda i: (i, 0)),
            pl.BlockSpec((tm, wb), lambda i: (i, 0)),
            pl.BlockSpec((wa, d), lambda i: (0, 0)),
            pl.BlockSpec((wb, d), lambda i: (1, 0)),
            pl.BlockSpec((tm, d), lambda i: (i, 0)),
            pl.BlockSpec((1, d), lambda i: (0, 0)),
        ],
        out_specs=[pl.BlockSpec((tm, d), lambda i: (i, 0)),
                   pl.BlockSpec((tm, d), lambda i: (i, 0))],
        out_shape=[jax.ShapeDtypeStruct((t, d), F32), jax.ShapeDtypeStruct((t, d), BF16)],
        compiler_params=pltpu.CompilerParams(
            dimension_semantics=("parallel",), vmem_limit_bytes=VMEM_LIMIT_BYTES),
        name="out_proj",
    )(oa, ob, w_out, w_out, x2, norm_w.reshape(1, d))


def _mlp_kernel(m_ref, wu_ref, wd_ref, h_ref, nw_ref, y_ref, acc_ref):
    j = pl.program_id(1)

    @pl.when(j == 0)
    def _():
        acc_ref[...] = h_ref[...]

    r = jnp.maximum(jnp.dot(m_ref[...], wu_ref[...], preferred_element_type=F32), 0.0)
    acc_ref[...] += jnp.dot((r * r).astype(BF16), wd_ref[...], preferred_element_type=F32)

    @pl.when(j == pl.num_programs(1) - 1)
    def _():
        y_ref[...] = _rms(acc_ref[...], nw_ref[...])


def _mlp(m, w_up, w_down, h, norm_w, *, tm, tf):
    t, d = m.shape
    f = w_up.shape[1]
    return pl.pallas_call(
        _mlp_kernel,
        grid=(t // tm, f // tf),
        in_specs=[pl.BlockSpec((tm, d), lambda i, j: (i, 0)),
                  pl.BlockSpec((d, tf), lambda i, j: (0, j)),
                  pl.BlockSpec((tf, d), lambda i, j: (j, 0)),
                  pl.BlockSpec((tm, d), lambda i, j: (i, 0)),
                  pl.BlockSpec((1, d), lambda i, j: (0, 0))],
        out_specs=pl.BlockSpec((tm, d), lambda i, j: (i, 0)),
        out_shape=jax.ShapeDtypeStruct((t, d), F32),
        scratch_shapes=[pltpu.VMEM((tm, d), F32)],
        compiler_params=pltpu.CompilerParams(
            dimension_semantics=("parallel", "arbitrary"), vmem_limit_bytes=VMEM_LIMIT_BYTES),
        name="mlp",
    )(m, w_up, w_down, h, norm_w.reshape(1, d))


def _block(x, attn_norm_w, w_in, lb_logits, sb_norm_w, hg_norm_w, w_out,
           mlp_norm_w, w_up, w_down, final_norm_w, *, tiles):
    b, s, d = x.shape
    depth = w_in.shape[0]
    sb_width = d // 2
    hg_width = d // 2
    x2 = x.reshape(b * s, d)
    h = x2
    for layer in range(depth):
        proj = _in_proj(h, attn_norm_w[layer], w_in[layer],
                        group_scales=(HEAD_DIM ** -0.5 * LOG2_E, 1.0, 1.0, 0.5, 0.5, 1.0, 0.5),
                        tm=tiles["proj_tm"], tn=tiles["proj_tn"])
        proj3 = proj.reshape(b, s, -1)
        o_a, w_up_bf, w_down_bf = _stick_break(proj3, sb_norm_w[layer], w_up[layer], w_down[layer],
                                               sb_width=sb_width, blk=tiles["sb_blk"],
                                               nprev=tiles["sb_prev"])
        o_b = _hgrn2(proj3, lb_logits, hg_norm_w[layer], col0=3 * sb_width, width=hg_width,
                     tb=tiles["hg_tb"], chunk=HEAD_DIM, layer=layer)
        h, m = _out_proj(o_a.reshape(b * s, sb_width), o_b.reshape(b * s, hg_width),
                         w_out[layer].astype(BF16), h, mlp_norm_w[layer], tm=tiles["out_tm"])
        assert layer == depth - 1, "only the single-layer block is fused end to end"
        h = _mlp(m, w_up_bf, w_down_bf, h, final_norm_w,
                 tm=tiles["mlp_tm"], tf=tiles["mlp_tf"])
    return h.reshape(b, s, d)


_TILES = dict(proj_tm=1024, proj_tn=1024, sb_blk=256, sb_prev=1, hg_tb=1024,
              out_tm=512, mlp_tm=512, mlp_tf=1024)


def kernel(x, attn_norm_w, w_in, lb_logits, sb_norm_w, hg_norm_w, w_out,
           mlp_norm_w, w_up, w_down, final_norm_w):
    return _block(x, attn_norm_w, w_in, lb_logits, sb_norm_w, hg_norm_w, w_out,
                  mlp_norm_w, w_up, w_down, final_norm_w, tiles=_TILES)
```

```python
import functools

import jax
import jax.numpy as jnp
from jax import lax
from jax.experimental import pallas as pl
from jax.experimental.pallas import tpu as pltpu

F32 = jnp.float32
BF16 = jnp.bfloat16

HEAD_DIM = 128
NORM_EPS = 1e-5
VMEM_LIMIT_BYTES = 56 * 1024 * 1024

LANES = 128
LOG2_E = 1.4426950408889634
SB_ZERO_LOG2 = -152.0
SB_NO_BLOCK_LOG2 = -1e30
HG_SAFE_LOG = 80.0
HG_CHUNKS_PER_ITER = 2

_NT = (((1,), (1,)), ((), ()))


def _rms(x, w):
    return x * lax.rsqrt(jnp.mean(x * x, axis=-1, keepdims=True) + NORM_EPS) * w


def _split_bf16(x):
    hi = x.astype(BF16)
    lo = (x - hi.astype(F32)).astype(BF16)
    return hi, lo


def _in_proj_kernel(x_ref, nw_ref, w_ref, o_ref, u_ref, *, col_scales):
    j = pl.program_id(1)

    @pl.when(j == 0)
    def _():
        u_ref[...] = _rms(x_ref[...], nw_ref[...]).astype(BF16)

    acc = jnp.dot(u_ref[...], w_ref[...].astype(BF16), preferred_element_type=F32)
    scale = jnp.float32(col_scales[0])
    for jj in range(1, len(col_scales)):
        scale = jnp.where(j == jj, jnp.float32(col_scales[jj]), scale)
    o_ref[...] = (acc * scale).astype(o_ref.dtype)


def _in_proj(x2, norm_w, w, *, group_scales, tm, tn):
    t, d = x2.shape
    n = w.shape[1]
    per_group = n // len(group_scales) // tn
    assert per_group * tn * len(group_scales) == n
    kern = functools.partial(
        _in_proj_kernel, col_scales=tuple(s for s in group_scales for _ in range(per_group)))
    return pl.pallas_call(
        kern,
        grid=(t // tm, n // tn),
        in_specs=[
            pl.BlockSpec((tm, d), lambda i, j: (i, 0)),
            pl.BlockSpec((1, d), lambda i, j: (0, 0)),
            pl.BlockSpec((d, tn), lambda i, j: (0, j)),
        ],
        out_specs=pl.BlockSpec((tm, tn), lambda i, j: (i, j)),
        out_shape=jax.ShapeDtypeStruct((t, n), BF16),
        scratch_shapes=[pltpu.VMEM((tm, d), BF16)],
        compiler_params=pltpu.CompilerParams(
            dimension_semantics=("parallel", "arbitrary"), vmem_limit_bytes=VMEM_LIMIT_BYTES),
        name="in_proj",
    )(x2, norm_w.reshape(1, d), w)


def _sb_kernel(*refs, heads, blk, nprev, k_col, v_col):
    q_ref = refs[0]
    k_refs = refs[1:2 + nprev]
    v_refs = refs[2 + nprev:3 + 2 * nprev]
    (proj_hbm, nw_ref, wu_ref, wd_ref, o_ref, wub_ref, wdb_ref,
     acc_ref, carry_ref, kbuf, vbuf, sem) = refs[3 + 2 * nprev:]
    b = pl.program_id(0)
    i = pl.program_id(1)
    d = HEAD_DIM
    sub = LANES
    nsub = blk // sub
    width = heads * d

    wub_ref[...] = wu_ref[...].astype(BF16)
    wdb_ref[...] = wd_ref[...].astype(BF16)

    r = lax.broadcasted_iota(jnp.int32, (2 * sub, 2 * sub), 0) % sub
    c = lax.broadcasted_iota(jnp.int32, (2 * sub, 2 * sub), 1)
    wmat = jnp.where((c >= sub) | (r > c), -1.0, 0.0).astype(BF16)
    tri = (lax.broadcasted_iota(jnp.int32, (blk, blk), 1)
           < lax.broadcasted_iota(jnp.int32, (blk, blk), 0))

    hsl = [slice(h * d, (h + 1) * d) for h in range(heads)]
    tri_all = jnp.concatenate([tri] * heads, axis=0)

    def diag_only(x, fill):
        return jnp.concatenate([x[:, :-blk], jnp.where(tri_all, x[:, -blk:], fill)], axis=1)

    def tiles(srcs, diag, older_bias=()):
        nb = len(srcs)
        z = jnp.concatenate(
            [jnp.concatenate(
                [lax.dot_general(q_ref[0, :, hs], k_src[:, hs], _NT, preferred_element_type=F32)
                 for k_src, _ in srcs], axis=1)
             for hs in hsl], axis=0)
        sp = jnp.maximum(z, 0.0) + jnp.log2(1.0 + jnp.exp2(-jnp.abs(z)))
        log_beta = z - sp
        if diag:
            sp = diag_only(sp, 0.0)
        hi, lo = _split_bf16(sp)
        carry = None if diag else carry_ref[...]
        after = [None] * (nb * nsub)
        for cb in reversed(range(nb * nsub)):
            cs = slice(cb * sub, (cb + 1) * sub)
            sums = jnp.dot(jnp.concatenate([hi[:, cs], lo[:, cs]], axis=1), wmat,
                           preferred_element_type=F32)
            after[cb] = sums[:, :sub] if carry is None else sums[:, :sub] + carry
            carry = sums[:, sub:] if carry is None else carry + sums[:, sub:]
            if diag and cb % nsub == 0 and 0 < cb // nsub < nb:
                carry = carry + older_bias[nb - 1 - cb // nsub]
        carry_ref[...] = carry
        w = jnp.exp2(log_beta + jnp.concatenate(after, axis=1))
        if diag:
            w = diag_only(w, 0.0)
        wb = w.astype(BF16)
        for h, hs in enumerate(hsl):
            pv = None
            for n, (_, v_src) in enumerate(srcs):
                part = jnp.dot(wb[h * blk:(h + 1) * blk, n * blk:(n + 1) * blk], v_src[:, hs],
                               preferred_element_type=F32)
                pv = part if pv is None else pv + part
            if diag:
                acc_ref[h] = pv
            else:
                acc_ref[h] += pv

    missing = [jnp.where(i >= k, 0.0, SB_NO_BLOCK_LOG2).astype(F32) for k in range(1, nprev + 1)]
    tiles([(k_refs[k].at[0], v_refs[k].at[0]) for k in reversed(range(nprev + 1))], True, missing)

    def fetch(j):
        rows = pl.ds(pl.multiple_of(j * blk, blk), blk)
        return (pltpu.make_async_copy(proj_hbm.at[b, rows, pl.ds(k_col, width)], kbuf, sem.at[0]),
                pltpu.make_async_copy(proj_hbm.at[b, rows, pl.ds(v_col, width)], vbuf, sem.at[1]))

    def cond(state):
        j, live = state
        return jnp.logical_and(j >= 0, live > SB_ZERO_LOG2)

    def body(state):
        j, _ = state
        for cp in fetch(j):
            cp.start()
        for cp in fetch(j):
            cp.wait()
        tiles([(kbuf, vbuf)], False)
        return j - 1, jnp.max(carry_ref[...])

    lax.while_loop(cond, body, (i - 1 - nprev, jnp.max(carry_ref[...])))

    nw = nw_ref[...]
    for h in range(heads):
        o_ref[0, :, h * d:(h + 1) * d] = _rms(acc_ref[h], nw).astype(o_ref.dtype)


def _stick_break(proj3, norm_w, w_up, w_down, *, sb_width, blk, nprev):
    b, s, _ = proj3.shape
    heads = sb_width // HEAD_DIM
    nq = s // blk
    steps = b * nq
    d_model, d_ff = w_up.shape
    ru, rd = d_model // steps, d_ff // steps
    assert ru * steps == d_model and rd * steps == d_ff
    kern = functools.partial(_sb_kernel, heads=heads, blk=blk, nprev=nprev,
                             k_col=sb_width, v_col=2 * sb_width)

    def back(k, col):
        return pl.BlockSpec((1, blk, sb_width), lambda bb, i: (bb, jnp.maximum(i - k, 0), col))

    return pl.pallas_call(
        kern,
        grid=(b, nq),
        in_specs=[
            back(0, 0),
            *[back(k, 1) for k in range(nprev + 1)],
            *[back(k, 2) for k in range(nprev + 1)],
            pl.BlockSpec(memory_space=pl.ANY),
            pl.BlockSpec((1, HEAD_DIM), lambda bb, i: (0, 0)),
            pl.BlockSpec((ru, d_ff), lambda bb, i: (bb * nq + i, 0)),
            pl.BlockSpec((rd, d_model), lambda bb, i: (bb * nq + i, 0)),
        ],
        out_specs=[pl.BlockSpec((1, blk, sb_width), lambda bb, i: (bb, i, 0)),
                   pl.BlockSpec((ru, d_ff), lambda bb, i: (bb * nq + i, 0)),
                   pl.BlockSpec((rd, d_model), lambda bb, i: (bb * nq + i, 0))],
        out_shape=[jax.ShapeDtypeStruct((b, s, sb_width), BF16),
                   jax.ShapeDtypeStruct(w_up.shape, BF16),
                   jax.ShapeDtypeStruct(w_down.shape, BF16)],
        scratch_shapes=[pltpu.VMEM((heads, blk, HEAD_DIM), F32),
                        pltpu.VMEM((heads * blk, LANES), F32),
                        pltpu.VMEM((blk, sb_width), BF16),
                        pltpu.VMEM((blk, sb_width), BF16),
                        pltpu.SemaphoreType.DMA((2,))],
        compiler_params=pltpu.CompilerParams(
            dimension_semantics=("parallel", "arbitrary"), vmem_limit_bytes=VMEM_LIMIT_BYTES),
        name="stick_break",
    )(*([proj3] * (4 + 2 * nprev)), norm_w.reshape(1, HEAD_DIM), w_up, w_down)


def _hgrn_kernel(q_ref, f_ref, i_ref, g_ref, lbl_ref, nw_ref, o_ref,
                 st_ref, cum_ref, *, heads, chunk, layer):
    d = HEAD_DIM
    n_chunks = q_ref.shape[1] // chunk
    cpi = HG_CHUNKS_PER_ITER
    assert n_chunks % cpi == 0
    tb = pl.program_id(1)

    @pl.when(tb == 0)
    def _():
        st_ref[...] = jnp.zeros_like(st_ref)

    lbl = lbl_ref[...]
    e = jnp.exp(lbl - jnp.max(lbl, axis=0, keepdims=True))
    lb_all = jnp.sum(e[:layer + 1], axis=0, keepdims=True) / jnp.sum(e, axis=0, keepdims=True)
    nw = nw_ref[...]

    rr = lax.broadcasted_iota(jnp.int32, (chunk, chunk), 0)
    cc = lax.broadcasted_iota(jnp.int32, (chunk, chunk), 1)
    incl = cc <= rr
    rr2 = lax.broadcasted_iota(jnp.int32, (chunk, 2 * chunk), 0)
    cc2 = lax.broadcasted_iota(jnp.int32, (chunk, 2 * chunk), 1) % chunk
    tril2 = jnp.where(cc2 <= rr2, 1.0, 0.0).astype(BF16)

    def gates(c0, h):
        hs = slice(h * d, (h + 1) * d)
        rows = pl.ds(c0, chunk)
        lb = lb_all[:, hs]
        t1 = (0.5 * (1.0 - lb)) * jnp.tanh(f_ref[0, rows, hs].astype(F32))
        log_f = jnp.log(0.5 * (1.0 + lb) + t1)
        k_in = 0.5 * (1.0 - lb) - t1
        hq = q_ref[0, rows, hs].astype(F32)
        qs = hq + hq * jnp.tanh(hq)
        return qs, k_in, log_f

    def finish(c0, h, out):
        hs = slice(h * d, (h + 1) * d)
        rows = pl.ds(c0, chunk)
        hg = g_ref[0, rows, hs].astype(F32)
        gate = hg + hg * jnp.tanh(hg)
        o_ref[0, rows, hs] = (_rms(out, nw) * gate).astype(o_ref.dtype)

    def fixup(c0, u):
        ones8 = jnp.ones((8, d), BF16)
        scores = cum_ref.at[cpi * heads]
        for h in range(heads):
            hs = slice(h * d, (h + 1) * d)
            qs, k_in, _unused = gates(c0, h)
            cum = cum_ref[u * heads + h]
            v = i_ref[0, pl.ds(c0, chunk), hs]

            def row(t, _c):
                sel = rr[:, :1] == t
                q_t = jnp.sum(jnp.where(sel, qs, 0.0), axis=0, keepdims=True)
                cum_t = jnp.sum(jnp.where(sel, cum, 0.0), axis=0, keepdims=True)
                p = q_t * k_in * jnp.exp(jnp.minimum(cum_t - cum, 0.0))
                srow = lax.dot_general(ones8, p.astype(BF16), _NT,
                                       preferred_element_type=F32)
                scores[pl.ds(t, 1), :] = jnp.where(cc[:1, :] <= t, srow[:1, :], 0.0)
                return _c

            lax.fori_loop(0, chunk, row, 0)
            qg = (qs * jnp.exp(cum)).astype(BF16)
            out = (jnp.dot(scores[...].astype(BF16), v, preferred_element_type=F32)
                   + lax.dot_general(qg, st_ref[u, h].astype(BF16), _NT,
                                     preferred_element_type=F32))
            finish(c0, h, out)

    def iter_body(it, _):
        hr = range(heads)
        hsl = [slice(h * d, (h + 1) * d) for h in hr]
        c0 = [pl.multiple_of((it * cpi + u) * chunk, chunk) for u in range(cpi)]
        uh = [(u, h) for u in range(cpi) for h in hr]
        g = {k: gates(c0[k[0]], k[1]) for k in uh}
        cum = {}
        for u, h in uh:
            hi, lo = _split_bf16(g[u, h][2])
            cum[u, h] = jnp.dot(tril2, jnp.concatenate([hi, lo], axis=0),
                                preferred_element_type=F32)
            cum_ref[u * heads + h] = cum[u, h]
        last = {k: cum[k][chunk - 1:chunk, :] for k in uh}
        qg = {k: (g[k][0] * jnp.exp(cum[k])).astype(BF16) for k in uh}
        kd = {k: g[k][1] * jnp.exp(last[k] - cum[k]) for k in uh}
        kg = {k: (kd[k] * jnp.exp(jnp.minimum(-last[k], HG_SAFE_LOG))).astype(BF16) for k in uh}
        v = {(u, h): i_ref[0, pl.ds(c0[u], chunk), hsl[h]] for u, h in uh}
        a = {k: lax.dot_general(qg[k], kg[k], _NT, preferred_element_type=F32) for k in uh}
        st = [st_ref[0, h] for h in hr]
        inter = {}
        for u in range(cpi):
            for h in hr:
                inter[u, h] = lax.dot_general(qg[u, h], st[h].astype(BF16), _NT,
                                              preferred_element_type=F32)
            for h in hr:
                vt = v[u, h].astype(F32).T.astype(BF16)
                st[h] = (st[h] * jnp.exp(last[u, h])
                         + jnp.dot(vt, kd[u, h].astype(BF16), preferred_element_type=F32))
                st_ref[u + 1, h] = st[h]
        for u, h in uh:
            am = jnp.where(incl, a[u, h], 0.0).astype(BF16)
            finish(c0[u], h, jnp.dot(am, v[u, h], preferred_element_type=F32) + inter[u, h])
        worst = [functools.reduce(jnp.minimum, [jnp.min(last[u, h]) for h in hr])
                 for u in range(cpi)]

        for u in range(cpi):
            pl.when(worst[u] < -HG_SAFE_LOG)(functools.partial(fixup, c0[u], u))

        st_ref[0] = st_ref[cpi]
        return 0

    lax.fori_loop(0, n_chunks // cpi, iter_body, 0)


def _hgrn2(proj3, lb_logits, norm_w, *, col0, width, tb, chunk, layer):
    b, s, _ = proj3.shape
    heads = width // HEAD_DIM
    assert chunk == HEAD_DIM, "score scratch rows reuse the (chunk, HEAD_DIM) cum buffers"
    cb = col0 // width
    kern = functools.partial(_hgrn_kernel, heads=heads, chunk=chunk, layer=layer)

    def col(k):
        return pl.BlockSpec((1, tb, width), lambda bb, t: (bb, t, cb + k))

    return pl.pallas_call(
        kern,
        grid=(b, s // tb),
        in_specs=[col(0), col(1), col(2), col(3),
                  pl.BlockSpec(lb_logits.shape, lambda bb, t: (0, 0)),
                  pl.BlockSpec((1, HEAD_DIM), lambda bb, t: (0, 0))],
        out_specs=pl.BlockSpec((1, tb, width), lambda bb, t: (bb, t, 0)),
        out_shape=jax.ShapeDtypeStruct((b, s, width), BF16),
        scratch_shapes=[pltpu.VMEM((HG_CHUNKS_PER_ITER + 1, heads, HEAD_DIM, HEAD_DIM), F32),
                        pltpu.VMEM((HG_CHUNKS_PER_ITER * heads + 1, chunk, HEAD_DIM), F32)],
        compiler_params=pltpu.CompilerParams(
            dimension_semantics=("parallel", "arbitrary"), vmem_limit_bytes=VMEM_LIMIT_BYTES),
        name="hgrn2",
    )(proj3, proj3, proj3, proj3, lb_logits, norm_w.reshape(1, HEAD_DIM))


def _out_proj_kernel(oa_ref, ob_ref, wa_ref, wb_ref, x_ref, nw_ref, h_ref, m_ref):
    acc = (jnp.dot(oa_ref[...], wa_ref[...], preferred_element_type=F32)
           + jnp.dot(ob_ref[...], wb_ref[...], preferred_element_type=F32))
    h = x_ref[...] + acc
    h_ref[...] = h
    m_ref[...] = _rms(h, nw_ref[...]).astype(m_ref.dtype)


def _out_proj(oa, ob, w_out, x2, norm_w, *, tm):
    t, d = x2.shape
    wa = oa.shape[1]
    wb = ob.shape[1]
    assert wa == wb
    return pl.pallas_call(
        _out_proj_kernel,
        grid=(t // tm,),
        in_specs=[
            pl.BlockSpec((tm, wa), lambda i: (i, 0)),
            pl.BlockSpec((tm, wb), lambda i: (i, 0)),
            pl.BlockSpec((wa, d), lambda i: (0, 0)),
            pl.BlockSpec((wb, d), lambda i: (1, 0)),
            pl.BlockSpec((tm, d), lambda i: (i, 0)),
            pl.BlockSpec((1, d), lambda i: (0, 0)),
        ],
        out_specs=[pl.BlockSpec((tm, d), lambda i: (i, 0)),
                   pl.BlockSpec((tm, d), lambda i: (i, 0))],
        out_shape=[jax.ShapeDtypeStruct((t, d), F32), jax.ShapeDtypeStruct((t, d), BF16)],
        compiler_params=pltpu.CompilerParams(
            dimension_semantics=("parallel",), vmem_limit_bytes=VMEM_LIMIT_BYTES),
        name="out_proj",
    )(oa, ob, w_out, w_out, x2, norm_w.reshape(1, d))


def _mlp_kernel(m_ref, wu_ref, wd_ref, h_hbm, nw_ref, y_hbm, acc_ref, sem):
    i = pl.program_id(0)
    j = pl.program_id(1)
    ni = pl.num_programs(0)
    nj = pl.num_programs(1)
    tm = acc_ref.shape[1]
    s = i % 2

    def h_copy(blk, slot):
        return pltpu.make_async_copy(h_hbm.at[pl.ds(blk * tm, tm)], acc_ref.at[slot],
                                     sem.at[0, slot])

    def y_copy(blk, slot):
        return pltpu.make_async_copy(acc_ref.at[slot], y_hbm.at[pl.ds(blk * tm, tm)],
                                     sem.at[1, slot])

    @pl.when(jnp.logical_and(i == 0, j == 0))
    def _():
        h_copy(0, 0).start()

    @pl.when(j == 0)
    def _():
        h_copy(i, s).wait()

    @pl.when(j == 1)
    def _():
        @pl.when(i >= 1)
        def _():
            y_copy(i - 1, 1 - s).wait()

        @pl.when(i + 1 < ni)
        def _():
            h_copy(i + 1, 1 - s).start()

    r = jnp.maximum(jnp.dot(m_ref[...], wu_ref[...], preferred_element_type=F32), 0.0)
    acc_ref[s] += jnp.dot((r * r).astype(BF16), wd_ref[...], preferred_element_type=F32)

    @pl.when(j == nj - 1)
    def _():
        acc_ref[s] = _rms(acc_ref[s], nw_ref[...])
        y_copy(i, s).start()

        @pl.when(i == ni - 1)
        def _():
            y_copy(i, s).wait()


def _mlp(m, w_up, w_down, h, norm_w, *, tm, tf):
    t, d = m.shape
    f = w_up.shape[1]
    assert f // tf >= 2, "the slot hand-over happens at the second inner step"
    return pl.pallas_call(
        _mlp_kernel,
        grid=(t // tm, f // tf),
        in_specs=[pl.BlockSpec((tm, d), lambda i, j: (i, 0)),
                  pl.BlockSpec((d, tf), lambda i, j: (0, j)),
                  pl.BlockSpec((tf, d), lambda i, j: (j, 0)),
                  pl.BlockSpec(memory_space=pl.ANY),
                  pl.BlockSpec((1, d), lambda i, j: (0, 0))],
        out_specs=pl.BlockSpec(memory_space=pl.ANY),
        out_shape=jax.ShapeDtypeStruct((t, d), F32),
        scratch_shapes=[pltpu.VMEM((2, tm, d), F32), pltpu.SemaphoreType.DMA((2, 2))],
        compiler_params=pltpu.CompilerParams(
            dimension_semantics=("arbitrary", "arbitrary"), vmem_limit_bytes=VMEM_LIMIT_BYTES),
        name="mlp",
    )(m, w_up, w_down, h, norm_w.reshape(1, d))


def _block(x, attn_norm_w, w_in, lb_logits, sb_norm_w, hg_norm_w, w_out,
           mlp_norm_w, w_up, w_down, final_norm_w, *, tiles):
    b, s, d = x.shape
    depth = w_in.shape[0]
    sb_width = d // 2
    hg_width = d // 2
    x2 = x.reshape(b * s, d)
    h = x2
    for layer in range(depth):
        proj = _in_proj(h, attn_norm_w[layer], w_in[layer],
                        group_scales=(HEAD_DIM ** -0.5 * LOG2_E, 1.0, 1.0, 0.5, 0.5, 1.0, 0.5),
                        tm=tiles["proj_tm"], tn=tiles["proj_tn"])
        proj3 = proj.reshape(b, s, -1)
        o_a, w_up_bf, w_down_bf = _stick_break(proj3, sb_norm_w[layer], w_up[layer], w_down[layer],
                                               sb_width=sb_width, blk=tiles["sb_blk"],
                                               nprev=tiles["sb_prev"])
        o_b = _hgrn2(proj3, lb_logits, hg_norm_w[layer], col0=3 * sb_width, width=hg_width,
                     tb=tiles["hg_tb"], chunk=HEAD_DIM, layer=layer)
        h, m = _out_proj(o_a.reshape(b * s, sb_width), o_b.reshape(b * s, hg_width),
                         w_out[layer].astype(BF16), h, mlp_norm_w[layer], tm=tiles["out_tm"])
        assert layer == depth - 1, "only the single-layer block is fused end to end"
        h = _mlp(m, w_up_bf, w_down_bf, h, final_norm_w,
                 tm=tiles["mlp_tm"], tf=tiles["mlp_tf"])
    return h.reshape(b, s, d)


_TILES = dict(proj_tm=1024, proj_tn=1024, sb_blk=256, sb_prev=1, hg_tb=1024,
              out_tm=512, mlp_tm=1024, mlp_tf=1024)


def kernel(x, attn_norm_w, w_in, lb_logits, sb_norm_w, hg_norm_w, w_out,
           mlp_norm_w, w_up, w_down, final_norm_w):
    return _block(x, attn_norm_w, w_in, lb_logits, sb_norm_w, hg_norm_w, w_out,
                  mlp_norm_w, w_up, w_down, final_norm_w, tiles=_TILES)
```

```python
import functools

import jax
import jax.numpy as jnp
from jax import lax
from jax.experimental import pallas as pl
from jax.experimental.pallas import tpu as pltpu

F32 = jnp.float32
BF16 = jnp.bfloat16

HEAD_DIM = 128
NORM_EPS = 1e-5
VMEM_LIMIT_BYTES = 56 * 1024 * 1024

LANES = 128
LOG2_E = 1.4426950408889634
SB_ZERO_LOG2 = -152.0
SB_NO_BLOCK_LOG2 = -1e30
HG_SAFE_LOG = 80.0
HG_CHUNKS_PER_ITER = 2

_NT = (((1,), (1,)), ((), ()))


def _rms(x, w):
    return x * lax.rsqrt(jnp.mean(x * x, axis=-1, keepdims=True) + NORM_EPS) * w


def _split_bf16(x):
    hi = x.astype(BF16)
    lo = (x - hi.astype(F32)).astype(BF16)
    return hi, lo


def _in_proj_kernel(x_hbm, nw_ref, w_ref, o_ref, xbuf, u_ref, sem, *, col_scales):
    i = pl.program_id(0)
    j = pl.program_id(1)
    tm = xbuf.shape[0]

    def x_copy(blk):
        return pltpu.make_async_copy(x_hbm.at[pl.ds(blk * tm, tm)], xbuf, sem.at[0])

    @pl.when(j == 0)
    def _():
        @pl.when(i == 0)
        def _():
            x_copy(0).start()

        x_copy(i).wait()
        u_ref[...] = _rms(xbuf[...], nw_ref[...]).astype(BF16)

        @pl.when(i + 1 < pl.num_programs(0))
        def _():
            x_copy(i + 1).start()

    acc = jnp.dot(u_ref[...], w_ref[...].astype(BF16), preferred_element_type=F32)
    scale = jnp.float32(col_scales[0])
    for jj in range(1, len(col_scales)):
        scale = jnp.where(j == jj, jnp.float32(col_scales[jj]), scale)
    o_ref[...] = (acc * scale).astype(o_ref.dtype)


def _in_proj(x2, norm_w, w, *, group_scales, tm, tn):
    t, d = x2.shape
    n = w.shape[1]
    per_group = n // len(group_scales) // tn
    assert per_group * tn * len(group_scales) == n
    kern = functools.partial(
        _in_proj_kernel, col_scales=tuple(s for s in group_scales for _ in range(per_group)))
    return pl.pallas_call(
        kern,
        grid=(t // tm, n // tn),
        in_specs=[
            pl.BlockSpec(memory_space=pl.ANY),
            pl.BlockSpec((1, d), lambda i, j: (0, 0)),
            pl.BlockSpec((d, tn), lambda i, j: (0, j)),
        ],
        out_specs=pl.BlockSpec((tm, tn), lambda i, j: (i, j)),
        out_shape=jax.ShapeDtypeStruct((t, n), BF16),
        scratch_shapes=[pltpu.VMEM((tm, d), F32), pltpu.VMEM((tm, d), BF16),
                        pltpu.SemaphoreType.DMA((1,))],
        compiler_params=pltpu.CompilerParams(
            dimension_semantics=("arbitrary", "arbitrary"), vmem_limit_bytes=VMEM_LIMIT_BYTES),
        name="in_proj",
    )(x2, norm_w.reshape(1, d), w)


def _sb_kernel(*refs, heads, blk, nprev, ncast, k_col, v_col):
    q_ref = refs[0]
    k_refs = refs[1:2 + nprev]
    v_refs = refs[2 + nprev:3 + 2 * nprev]
    rest = refs[3 + 2 * nprev:]
    proj_hbm, nw_ref = rest[:2]
    cast_in = rest[2:2 + ncast]
    o_ref = rest[2 + ncast]
    cast_out = rest[3 + ncast:3 + 2 * ncast]
    acc_ref, carry_ref, kbuf, vbuf, sem = rest[3 + 2 * ncast:]
    b = pl.program_id(0)
    i = pl.program_id(1)
    d = HEAD_DIM
    sub = LANES
    nsub = blk // sub
    width = heads * d

    for src, dst in zip(cast_in, cast_out):
        dst[...] = src[...].astype(BF16)

    r = lax.broadcasted_iota(jnp.int32, (2 * sub, 2 * sub), 0) % sub
    c = lax.broadcasted_iota(jnp.int32, (2 * sub, 2 * sub), 1)
    wmat = jnp.where((c >= sub) | (r > c), -1.0, 0.0).astype(BF16)
    tri = (lax.broadcasted_iota(jnp.int32, (blk, blk), 1)
           < lax.broadcasted_iota(jnp.int32, (blk, blk), 0))

    hsl = [slice(h * d, (h + 1) * d) for h in range(heads)]
    tri_all = jnp.concatenate([tri] * heads, axis=0)

    def diag_only(x, fill):
        return jnp.concatenate([x[:, :-blk], jnp.where(tri_all, x[:, -blk:], fill)], axis=1)

    def tiles(srcs, diag, older_bias=()):
        nb = len(srcs)
        z = jnp.concatenate(
            [jnp.concatenate(
                [lax.dot_general(q_ref[0, :, hs], k_src[:, hs], _NT, preferred_element_type=F32)
                 for k_src, _ in srcs], axis=1)
             for hs in hsl], axis=0)
        sp = jnp.maximum(z, 0.0) + jnp.log2(1.0 + jnp.exp2(-jnp.abs(z)))
        log_beta = z - sp
        if diag:
            sp = diag_only(sp, 0.0)
        hi, lo = _split_bf16(sp)
        carry = None if diag else carry_ref[...]
        after = [None] * (nb * nsub)
        for cb in reversed(range(nb * nsub)):
            cs = slice(cb * sub, (cb + 1) * sub)
            sums = jnp.dot(jnp.concatenate([hi[:, cs], lo[:, cs]], axis=1), wmat,
                           preferred_element_type=F32)
            after[cb] = sums[:, :sub] if carry is None else sums[:, :sub] + carry
            carry = sums[:, sub:] if carry is None else carry + sums[:, sub:]
            if diag and cb % nsub == 0 and 0 < cb // nsub < nb:
                carry = carry + older_bias[nb - 1 - cb // nsub]
        carry_ref[...] = carry
        w = jnp.exp2(log_beta + jnp.concatenate(after, axis=1))
        if diag:
            w = diag_only(w, 0.0)
        wb = w.astype(BF16)
        for h, hs in enumerate(hsl):
            pv = None
            for n, (_, v_src) in enumerate(srcs):
                part = jnp.dot(wb[h * blk:(h + 1) * blk, n * blk:(n + 1) * blk], v_src[:, hs],
                               preferred_element_type=F32)
                pv = part if pv is None else pv + part
            if diag:
                acc_ref[h] = pv
            else:
                acc_ref[h] += pv

    missing = [jnp.where(i >= k, 0.0, SB_NO_BLOCK_LOG2).astype(F32) for k in range(1, nprev + 1)]
    tiles([(k_refs[k].at[0], v_refs[k].at[0]) for k in reversed(range(nprev + 1))], True, missing)

    def fetch(j):
        rows = pl.ds(pl.multiple_of(j * blk, blk), blk)
        return (pltpu.make_async_copy(proj_hbm.at[b, rows, pl.ds(k_col, width)], kbuf, sem.at[0]),
                pltpu.make_async_copy(proj_hbm.at[b, rows, pl.ds(v_col, width)], vbuf, sem.at[1]))

    def cond(state):
        j, live = state
        return jnp.logical_and(j >= 0, live > SB_ZERO_LOG2)

    def body(state):
        j, _ = state
        for cp in fetch(j):
            cp.start()
        for cp in fetch(j):
            cp.wait()
        tiles([(kbuf, vbuf)], False)
        return j - 1, jnp.max(carry_ref[...])

    lax.while_loop(cond, body, (i - 1 - nprev, jnp.max(carry_ref[...])))

    nw = nw_ref[...]
    for h in range(heads):
        o_ref[0, :, h * d:(h + 1) * d] = _rms(acc_ref[h], nw).astype(o_ref.dtype)


def _stick_break(proj3, norm_w, cast_weights, *, sb_width, blk, nprev):
    b, s, _ = proj3.shape
    heads = sb_width // HEAD_DIM
    nq = s // blk
    steps = b * nq
    for w in cast_weights:
        assert w.shape[0] % (steps * 16) == 0, "a bf16 slab needs a multiple of 16 rows"
    cast_specs = [pl.BlockSpec((w.shape[0] // steps, w.shape[1]), lambda bb, i: (bb * nq + i, 0))
                  for w in cast_weights]
    kern = functools.partial(_sb_kernel, heads=heads, blk=blk, nprev=nprev,
                             ncast=len(cast_weights), k_col=sb_width, v_col=2 * sb_width)

    def back(k, col):
        return pl.BlockSpec((1, blk, sb_width), lambda bb, i: (bb, jnp.maximum(i - k, 0), col))

    return pl.pallas_call(
        kern,
        grid=(b, nq),
        in_specs=[
            back(0, 0),
            *[back(k, 1) for k in range(nprev + 1)],
            *[back(k, 2) for k in range(nprev + 1)],
            pl.BlockSpec(memory_space=pl.ANY),
            pl.BlockSpec((1, HEAD_DIM), lambda bb, i: (0, 0)),
            *cast_specs,
        ],
        out_specs=[pl.BlockSpec((1, blk, sb_width), lambda bb, i: (bb, i, 0)), *cast_specs],
        out_shape=[jax.ShapeDtypeStruct((b, s, sb_width), BF16),
                   *[jax.ShapeDtypeStruct(w.shape, BF16) for w in cast_weights]],
        scratch_shapes=[pltpu.VMEM((heads, blk, HEAD_DIM), F32),
                        pltpu.VMEM((heads * blk, LANES), F32),
                        pltpu.VMEM((blk, sb_width), BF16),
                        pltpu.VMEM((blk, sb_width), BF16),
                        pltpu.SemaphoreType.DMA((2,))],
        compiler_params=pltpu.CompilerParams(
            dimension_semantics=("parallel", "arbitrary"), vmem_limit_bytes=VMEM_LIMIT_BYTES),
        name="stick_break",
    )(*([proj3] * (4 + 2 * nprev)), norm_w.reshape(1, HEAD_DIM), *cast_weights)


def _hgrn_kernel(q_ref, f_ref, i_ref, g_ref, lbl_ref, nw_ref, o_ref,
                 st_ref, cum_ref, *, heads, chunk, layer):
    d = HEAD_DIM
    n_chunks = q_ref.shape[1] // chunk
    cpi = HG_CHUNKS_PER_ITER
    assert n_chunks % cpi == 0
    tb = pl.program_id(1)

    @pl.when(tb == 0)
    def _():
        st_ref[...] = jnp.zeros_like(st_ref)

    lbl = lbl_ref[...]
    e = jnp.exp(lbl - jnp.max(lbl, axis=0, keepdims=True))
    lb_all = jnp.sum(e[:layer + 1], axis=0, keepdims=True) / jnp.sum(e, axis=0, keepdims=True)
    nw = nw_ref[...]

    rr = lax.broadcasted_iota(jnp.int32, (chunk, chunk), 0)
    cc = lax.broadcasted_iota(jnp.int32, (chunk, chunk), 1)
    incl = cc <= rr
    rr2 = lax.broadcasted_iota(jnp.int32, (chunk, 2 * chunk), 0)
    cc2 = lax.broadcasted_iota(jnp.int32, (chunk, 2 * chunk), 1) % chunk
    tril2 = jnp.where(cc2 <= rr2, 1.0, 0.0).astype(BF16)

    def gates(c0, h):
        hs = slice(h * d, (h + 1) * d)
        rows = pl.ds(c0, chunk)
        lb = lb_all[:, hs]
        t1 = (0.5 * (1.0 - lb)) * jnp.tanh(f_ref[0, rows, hs].astype(F32))
        log_f = jnp.log(0.5 * (1.0 + lb) + t1)
        k_in = 0.5 * (1.0 - lb) - t1
        hq = q_ref[0, rows, hs].astype(F32)
        qs = hq + hq * jnp.tanh(hq)
        return qs, k_in, log_f

    def finish(c0, h, out):
        hs = slice(h * d, (h + 1) * d)
        rows = pl.ds(c0, chunk)
        hg = g_ref[0, rows, hs].astype(F32)
        gate = hg + hg * jnp.tanh(hg)
        o_ref[0, rows, hs] = (_rms(out, nw) * gate).astype(o_ref.dtype)

    def fixup(c0, u):
        ones8 = jnp.ones((8, d), BF16)
        scores = cum_ref.at[cpi * heads]
        for h in range(heads):
            hs = slice(h * d, (h + 1) * d)
            qs, k_in, _unused = gates(c0, h)
            cum = cum_ref[u * heads + h]
            v = i_ref[0, pl.ds(c0, chunk), hs]

            def row(t, _c):
                sel = rr[:, :1] == t
                q_t = jnp.sum(jnp.where(sel, qs, 0.0), axis=0, keepdims=True)
                cum_t = jnp.sum(jnp.where(sel, cum, 0.0), axis=0, keepdims=True)
                p = q_t * k_in * jnp.exp(jnp.minimum(cum_t - cum, 0.0))
                srow = lax.dot_general(ones8, p.astype(BF16), _NT,
                                       preferred_element_type=F32)
                scores[pl.ds(t, 1), :] = jnp.where(cc[:1, :] <= t, srow[:1, :], 0.0)
                return _c

            lax.fori_loop(0, chunk, row, 0)
            qg = (qs * jnp.exp(cum)).astype(BF16)
            out = (jnp.dot(scores[...].astype(BF16), v, preferred_element_type=F32)
                   + lax.dot_general(qg, st_ref[u, h].astype(BF16), _NT,
                                     preferred_element_type=F32))
            finish(c0, h, out)

    def iter_body(it, _):
        hr = range(heads)
        hsl = [slice(h * d, (h + 1) * d) for h in hr]
        c0 = [pl.multiple_of((it * cpi + u) * chunk, chunk) for u in range(cpi)]
        uh = [(u, h) for u in range(cpi) for h in hr]
        g = {k: gates(c0[k[0]], k[1]) for k in uh}
        cum = {}
        for u, h in uh:
            hi, lo = _split_bf16(g[u, h][2])
            cum[u, h] = jnp.dot(tril2, jnp.concatenate([hi, lo], axis=0),
                                preferred_element_type=F32)
            cum_ref[u * heads + h] = cum[u, h]
        last = {k: cum[k][chunk - 1:chunk, :] for k in uh}
        qg = {k: (g[k][0] * jnp.exp(cum[k])).astype(BF16) for k in uh}
        kd = {k: g[k][1] * jnp.exp(last[k] - cum[k]) for k in uh}
        kg = {k: (kd[k] * jnp.exp(jnp.minimum(-last[k], HG_SAFE_LOG))).astype(BF16) for k in uh}
        v = {(u, h): i_ref[0, pl.ds(c0[u], chunk), hsl[h]] for u, h in uh}
        a = {k: lax.dot_general(qg[k], kg[k], _NT, preferred_element_type=F32) for k in uh}
        st = [st_ref[0, h] for h in hr]
        inter = {}
        for u in range(cpi):
            for h in hr:
                inter[u, h] = lax.dot_general(qg[u, h], st[h].astype(BF16), _NT,
                                              preferred_element_type=F32)
            for h in hr:
                vt = v[u, h].astype(F32).T.astype(BF16)
                st[h] = (st[h] * jnp.exp(last[u, h])
                         + jnp.dot(vt, kd[u, h].astype(BF16), preferred_element_type=F32))
                st_ref[u + 1, h] = st[h]
        for u, h in uh:
            am = jnp.where(incl, a[u, h], 0.0).astype(BF16)
            finish(c0[u], h, jnp.dot(am, v[u, h], preferred_element_type=F32) + inter[u, h])
        worst = [functools.reduce(jnp.minimum, [jnp.min(last[u, h]) for h in hr])
                 for u in range(cpi)]

        for u in range(cpi):
            pl.when(worst[u] < -HG_SAFE_LOG)(functools.partial(fixup, c0[u], u))

        st_ref[0] = st_ref[cpi]
        return 0

    lax.fori_loop(0, n_chunks // cpi, iter_body, 0)


def _hgrn2(proj3, lb_logits, norm_w, *, col0, width, tb, chunk, layer):
    b, s, _ = proj3.shape
    heads = width // HEAD_DIM
    assert chunk == HEAD_DIM, "score scratch rows reuse the (chunk, HEAD_DIM) cum buffers"
    cb = col0 // width
    kern = functools.partial(_hgrn_kernel, heads=heads, chunk=chunk, layer=layer)

    def col(k):
        return pl.BlockSpec((1, tb, width), lambda bb, t: (bb, t, cb + k))

    return pl.pallas_call(
        kern,
        grid=(b, s // tb),
        in_specs=[col(0), col(1), col(2), col(3),
                  pl.BlockSpec(lb_logits.shape, lambda bb, t: (0, 0)),
                  pl.BlockSpec((1, HEAD_DIM), lambda bb, t: (0, 0))],
        out_specs=pl.BlockSpec((1, tb, width), lambda bb, t: (bb, t, 0)),
        out_shape=jax.ShapeDtypeStruct((b, s, width), BF16),
        scratch_shapes=[pltpu.VMEM((HG_CHUNKS_PER_ITER + 1, heads, HEAD_DIM, HEAD_DIM), F32),
                        pltpu.VMEM((HG_CHUNKS_PER_ITER * heads + 1, chunk, HEAD_DIM), F32)],
        compiler_params=pltpu.CompilerParams(
            dimension_semantics=("parallel", "arbitrary"), vmem_limit_bytes=VMEM_LIMIT_BYTES),
        name="hgrn2",
    )(proj3, proj3, proj3, proj3, lb_logits, norm_w.reshape(1, HEAD_DIM))


def _out_proj_kernel(oa_ref, ob_ref, wa_ref, wb_ref, x_ref, nw_ref, h_ref, m_ref):
    acc = (jnp.dot(oa_ref[...], wa_ref[...], preferred_element_type=F32)
           + jnp.dot(ob_ref[...], wb_ref[...], preferred_element_type=F32))
    h = x_ref[...] + acc
    h_ref[...] = h
    m_ref[...] = _rms(h, nw_ref[...]).astype(m_ref.dtype)


def _out_proj(oa, ob, w_out, x2, norm_w, *, tm):
    t, d = x2.shape
    wa = oa.shape[1]
    wb = ob.shape[1]
    assert wa == wb
    return pl.pallas_call(
        _out_proj_kernel,
        grid=(t // tm,),
        in_specs=[
            pl.BlockSpec((tm, wa), lambda i: (i, 0)),
            pl.BlockSpec((tm, wb), lambda i: (i, 0)),
            pl.BlockSpec((wa, d), lambda i: (0, 0)),
            pl.BlockSpec((wb, d), lambda i: (1, 0)),
            pl.BlockSpec((tm, d), lambda i: (i, 0)),
            pl.BlockSpec((1, d), lambda i: (0, 0)),
        ],
        out_specs=[pl.BlockSpec((tm, d), lambda i: (i, 0)),
                   pl.BlockSpec((tm, d), lambda i: (i, 0))],
        out_shape=[jax.ShapeDtypeStruct((t, d), F32), jax.ShapeDtypeStruct((t, d), BF16)],
        compiler_params=pltpu.CompilerParams(
            dimension_semantics=("parallel",), vmem_limit_bytes=VMEM_LIMIT_BYTES),
        name="out_proj",
    )(oa, ob, w_out, w_out, x2, norm_w.reshape(1, d))


def _mlp_kernel(m_ref, wu_ref, wd_ref, h_hbm, nw_ref, y_hbm, acc_ref, sem):
    i = pl.program_id(0)
    j = pl.program_id(1)
    ni = pl.num_programs(0)
    nj = pl.num_programs(1)
    tm = acc_ref.shape[1]
    s = i % 2

    def h_copy(blk, slot):
        return pltpu.make_async_copy(h_hbm.at[pl.ds(blk * tm, tm)], acc_ref.at[slot],
                                     sem.at[0, slot])

    def y_copy(blk, slot):
        return pltpu.make_async_copy(acc_ref.at[slot], y_hbm.at[pl.ds(blk * tm, tm)],
                                     sem.at[1, slot])

    @pl.when(jnp.logical_and(i == 0, j == 0))
    def _():
        h_copy(0, 0).start()

    @pl.when(j == 0)
    def _():
        h_copy(i, s).wait()

    @pl.when(j == 1)
    def _():
        @pl.when(i >= 1)
        def _():
            y_copy(i - 1, 1 - s).wait()

        @pl.when(i + 1 < ni)
        def _():
            h_copy(i + 1, 1 - s).start()

    r = jnp.maximum(jnp.dot(m_ref[...], wu_ref[...], preferred_element_type=F32), 0.0)
    acc_ref[s] += jnp.dot((r * r).astype(BF16), wd_ref[...], preferred_element_type=F32)

    @pl.when(j == nj - 1)
    def _():
        acc_ref[s] = _rms(acc_ref[s], nw_ref[...])
        y_copy(i, s).start()

        @pl.when(i == ni - 1)
        def _():
            y_copy(i, s).wait()


def _mlp(m, w_up, w_down, h, norm_w, *, tm, tf):
    t, d = m.shape
    f = w_up.shape[1]
    assert f // tf >= 2, "the slot hand-over happens at the second inner step"
    return pl.pallas_call(
        _mlp_kernel,
        grid=(t // tm, f // tf),
        in_specs=[pl.BlockSpec((tm, d), lambda i, j: (i, 0)),
                  pl.BlockSpec((d, tf), lambda i, j: (0, j)),
                  pl.BlockSpec((tf, d), lambda i, j: (j, 0)),
                  pl.BlockSpec(memory_space=pl.ANY),
                  pl.BlockSpec((1, d), lambda i, j: (0, 0))],
        out_specs=pl.BlockSpec(memory_space=pl.ANY),
        out_shape=jax.ShapeDtypeStruct((t, d), F32),
        scratch_shapes=[pltpu.VMEM((2, tm, d), F32), pltpu.SemaphoreType.DMA((2, 2))],
        compiler_params=pltpu.CompilerParams(
            dimension_semantics=("arbitrary", "arbitrary"), vmem_limit_bytes=VMEM_LIMIT_BYTES),
        name="mlp",
    )(m, w_up, w_down, h, norm_w.reshape(1, d))


def _block(x, attn_norm_w, w_in, lb_logits, sb_norm_w, hg_norm_w, w_out,
           mlp_norm_w, w_up, w_down, final_norm_w, *, tiles):
    b, s, d = x.shape
    depth = w_in.shape[0]
    sb_width = d // 2
    hg_width = d // 2
    x2 = x.reshape(b * s, d)
    h = x2
    for layer in range(depth):
        proj = _in_proj(h, attn_norm_w[layer], w_in[layer],
                        group_scales=(HEAD_DIM ** -0.5 * LOG2_E, 1.0, 1.0, 0.5, 0.5, 1.0, 0.5),
                        tm=tiles["proj_tm"], tn=tiles["proj_tn"])
        proj3 = proj.reshape(b, s, -1)
        o_a, w_out_bf, w_up_bf, w_down_bf = _stick_break(
            proj3, sb_norm_w[layer], (w_out[layer], w_up[layer], w_down[layer]),
            sb_width=sb_width, blk=tiles["sb_blk"], nprev=tiles["sb_prev"])
        o_b = _hgrn2(proj3, lb_logits, hg_norm_w[layer], col0=3 * sb_width, width=hg_width,
                     tb=tiles["hg_tb"], chunk=HEAD_DIM, layer=layer)
        h, m = _out_proj(o_a.reshape(b * s, sb_width), o_b.reshape(b * s, hg_width),
                         w_out_bf, h, mlp_norm_w[layer], tm=tiles["out_tm"])
        assert layer == depth - 1, "only the single-layer block is fused end to end"
        h = _mlp(m, w_up_bf, w_down_bf, h, final_norm_w,
                 tm=tiles["mlp_tm"], tf=tiles["mlp_tf"])
    return h.reshape(b, s, d)


_TILES = dict(proj_tm=2048, proj_tn=512, sb_blk=256, sb_prev=1, hg_tb=1024,
              out_tm=512, mlp_tm=1024, mlp_tf=1024)


def kernel(x, attn_norm_w, w_in, lb_logits, sb_norm_w, hg_norm_w, w_out,
           mlp_norm_w, w_up, w_down, final_norm_w):
    return _block(x, attn_norm_w, w_in, lb_logits, sb_norm_w, hg_norm_w, w_out,
                  mlp_norm_w, w_up, w_down, final_norm_w, tiles=_TILES)
```

```python
import functools

import jax
import jax.numpy as jnp
from jax import lax
from jax.experimental import pallas as pl
from jax.experimental.pallas import tpu as pltpu

F32 = jnp.float32
BF16 = jnp.bfloat16

HEAD_DIM = 128
NORM_EPS = 1e-5
VMEM_LIMIT_BYTES = 56 * 1024 * 1024

LANES = 128
LOG2_E = 1.4426950408889634
SB_ZERO_LOG2 = -152.0
SB_NO_BLOCK_LOG2 = -1e30
HG_SAFE_LOG = 80.0
HG_CHUNKS_PER_ITER = 2

_NT = (((1,), (1,)), ((), ()))


def _rms(x, w):
    return x * lax.rsqrt(jnp.mean(x * x, axis=-1, keepdims=True) + NORM_EPS) * w


def _split_bf16(x):
    hi = x.astype(BF16)
    lo = (x - hi.astype(F32)).astype(BF16)
    return hi, lo


def _in_proj_kernel(x_hbm, nw_ref, w_ref, o_ref, xbuf, u_ref, sem, *, col_scales):
    i = pl.program_id(0)
    j = pl.program_id(1)
    tm = xbuf.shape[0]

    def x_copy(blk):
        return pltpu.make_async_copy(x_hbm.at[pl.ds(blk * tm, tm)], xbuf, sem.at[0])

    @pl.when(j == 0)
    def _():
        @pl.when(i == 0)
        def _():
            x_copy(0).start()

        x_copy(i).wait()
        u_ref[...] = _rms(xbuf[...], nw_ref[...]).astype(BF16)

        @pl.when(i + 1 < pl.num_programs(0))
        def _():
            x_copy(i + 1).start()

    acc = jnp.dot(u_ref[...], w_ref[...].astype(BF16), preferred_element_type=F32)
    scale = jnp.float32(col_scales[0])
    for jj in range(1, len(col_scales)):
        scale = jnp.where(j == jj, jnp.float32(col_scales[jj]), scale)
    o_ref[...] = (acc * scale).astype(o_ref.dtype)


def _in_proj(x2, norm_w, w, *, group_scales, tm, tn):
    t, d = x2.shape
    n = w.shape[1]
    per_group = n // len(group_scales) // tn
    assert per_group * tn * len(group_scales) == n
    kern = functools.partial(
        _in_proj_kernel, col_scales=tuple(s for s in group_scales for _ in range(per_group)))
    return pl.pallas_call(
        kern,
        grid=(t // tm, n // tn),
        in_specs=[
            pl.BlockSpec(memory_space=pl.ANY),
            pl.BlockSpec((1, d), lambda i, j: (0, 0)),
            pl.BlockSpec((d, tn), lambda i, j: (0, j)),
        ],
        out_specs=pl.BlockSpec((tm, tn), lambda i, j: (i, j)),
        out_shape=jax.ShapeDtypeStruct((t, n), BF16),
        scratch_shapes=[pltpu.VMEM((tm, d), F32), pltpu.VMEM((tm, d), BF16),
                        pltpu.SemaphoreType.DMA((1,))],
        compiler_params=pltpu.CompilerParams(
            dimension_semantics=("arbitrary", "arbitrary"), vmem_limit_bytes=VMEM_LIMIT_BYTES),
        name="in_proj",
    )(x2, norm_w.reshape(1, d), w)


def _sb_kernel(*refs, heads, blk, nprev, ncast, k_col, v_col):
    q_ref = refs[0]
    k_refs = refs[1:2 + nprev]
    v_refs = refs[2 + nprev:3 + 2 * nprev]
    rest = refs[3 + 2 * nprev:]
    proj_hbm, nw_ref = rest[:2]
    cast_in = rest[2:2 + ncast]
    o_ref = rest[2 + ncast]
    cast_out = rest[3 + ncast:3 + 2 * ncast]
    acc_ref, carry_ref, kbuf, vbuf, sem = rest[3 + 2 * ncast:]
    b = pl.program_id(0)
    i = pl.program_id(1)
    d = HEAD_DIM
    sub = LANES
    nsub = blk // sub
    width = heads * d

    for src, dst in zip(cast_in, cast_out):
        dst[...] = src[...].astype(BF16)

    r = lax.broadcasted_iota(jnp.int32, (2 * sub, 2 * sub), 0) % sub
    c = lax.broadcasted_iota(jnp.int32, (2 * sub, 2 * sub), 1)
    wmat = jnp.where((c >= sub) | (r > c), -1.0, 0.0).astype(BF16)
    tri = (lax.broadcasted_iota(jnp.int32, (blk, blk), 1)
           < lax.broadcasted_iota(jnp.int32, (blk, blk), 0))

    hsl = [slice(h * d, (h + 1) * d) for h in range(heads)]
    tri_all = jnp.concatenate([tri] * heads, axis=0)

    def diag_only(x, fill):
        return jnp.concatenate([x[:, :-blk], jnp.where(tri_all, x[:, -blk:], fill)], axis=1)

    def tiles(srcs, diag, older_bias=()):
        nb = len(srcs)
        z = jnp.concatenate(
            [jnp.concatenate(
                [lax.dot_general(q_ref[0, :, hs], k_src[:, hs], _NT, preferred_element_type=F32)
                 for k_src, _ in srcs], axis=1)
             for hs in hsl], axis=0)
        sp = jnp.maximum(z, 0.0) + jnp.log2(1.0 + jnp.exp2(-jnp.abs(z)))
        log_beta = z - sp
        if diag:
            sp = diag_only(sp, 0.0)
        hi, lo = _split_bf16(sp)
        carry = None if diag else carry_ref[...]
        after = [None] * (nb * nsub)
        for cb in reversed(range(nb * nsub)):
            cs = slice(cb * sub, (cb + 1) * sub)
            sums = jnp.dot(jnp.concatenate([hi[:, cs], lo[:, cs]], axis=1), wmat,
                           preferred_element_type=F32)
            after[cb] = sums[:, :sub] if carry is None else sums[:, :sub] + carry
            carry = sums[:, sub:] if carry is None else carry + sums[:, sub:]
            if diag and cb % nsub == 0 and 0 < cb // nsub < nb:
                carry = carry + older_bias[nb - 1 - cb // nsub]
        carry_ref[...] = carry
        w = jnp.exp2(log_beta + jnp.concatenate(after, axis=1))
        if diag:
            w = diag_only(w, 0.0)
        wb = w.astype(BF16)
        for h, hs in enumerate(hsl):
            pv = None
            for n, (_, v_src) in enumerate(srcs):
                part = jnp.dot(wb[h * blk:(h + 1) * blk, n * blk:(n + 1) * blk], v_src[:, hs],
                               preferred_element_type=F32)
                pv = part if pv is None else pv + part
            if diag:
                acc_ref[h] = pv
            else:
                acc_ref[h] += pv

    missing = [jnp.where(i >= k, 0.0, SB_NO_BLOCK_LOG2).astype(F32) for k in range(1, nprev + 1)]
    tiles([(k_refs[k].at[0], v_refs[k].at[0]) for k in reversed(range(nprev + 1))], True, missing)

    def fetch(j):
        rows = pl.ds(pl.multiple_of(j * blk, blk), blk)
        return (pltpu.make_async_copy(proj_hbm.at[b, rows, pl.ds(k_col, width)], kbuf, sem.at[0]),
                pltpu.make_async_copy(proj_hbm.at[b, rows, pl.ds(v_col, width)], vbuf, sem.at[1]))

    def cond(state):
        j, live = state
        return jnp.logical_and(j >= 0, live > SB_ZERO_LOG2)

    def body(state):
        j, _ = state
        for cp in fetch(j):
            cp.start()
        for cp in fetch(j):
            cp.wait()
        tiles([(kbuf, vbuf)], False)
        return j - 1, jnp.max(carry_ref[...])

    lax.while_loop(cond, body, (i - 1 - nprev, jnp.max(carry_ref[...])))

    nw = nw_ref[...]
    for h in range(heads):
        o_ref[0, :, h * d:(h + 1) * d] = _rms(acc_ref[h], nw).astype(o_ref.dtype)


def _stick_break(proj3, norm_w, cast_weights, *, sb_width, blk, nprev):
    b, s, _ = proj3.shape
    heads = sb_width // HEAD_DIM
    nq = s // blk
    steps = b * nq
    for w in cast_weights:
        assert w.shape[0] % (steps * 16) == 0, "a bf16 slab needs a multiple of 16 rows"
    cast_specs = [pl.BlockSpec((w.shape[0] // steps, w.shape[1]), lambda bb, i: (bb * nq + i, 0))
                  for w in cast_weights]
    kern = functools.partial(_sb_kernel, heads=heads, blk=blk, nprev=nprev,
                             ncast=len(cast_weights), k_col=sb_width, v_col=2 * sb_width)

    def back(k, col):
        return pl.BlockSpec((1, blk, sb_width), lambda bb, i: (bb, jnp.maximum(i - k, 0), col))

    return pl.pallas_call(
        kern,
        grid=(b, nq),
        in_specs=[
            back(0, 0),
            *[back(k, 1) for k in range(nprev + 1)],
            *[back(k, 2) for k in range(nprev + 1)],
            pl.BlockSpec(memory_space=pl.ANY),
            pl.BlockSpec((1, HEAD_DIM), lambda bb, i: (0, 0)),
            *cast_specs,
        ],
        out_specs=[pl.BlockSpec((1, blk, sb_width), lambda bb, i: (bb, i, 0)), *cast_specs],
        out_shape=[jax.ShapeDtypeStruct((b, s, sb_width), BF16),
                   *[jax.ShapeDtypeStruct(w.shape, BF16) for w in cast_weights]],
        scratch_shapes=[pltpu.VMEM((heads, blk, HEAD_DIM), F32),
                        pltpu.VMEM((heads * blk, LANES), F32),
                        pltpu.VMEM((blk, sb_width), BF16),
                        pltpu.VMEM((blk, sb_width), BF16),
                        pltpu.SemaphoreType.DMA((2,))],
        compiler_params=pltpu.CompilerParams(
            dimension_semantics=("parallel", "arbitrary"), vmem_limit_bytes=VMEM_LIMIT_BYTES),
        name="stick_break",
    )(*([proj3] * (4 + 2 * nprev)), norm_w.reshape(1, HEAD_DIM), *cast_weights)


def _hgrn_kernel(q_ref, f_ref, i_ref, g_ref, lbl_ref, nw_ref, o_ref,
                 st_ref, cum_ref, *, heads, chunk, layer):
    d = HEAD_DIM
    n_chunks = q_ref.shape[1] // chunk
    cpi = HG_CHUNKS_PER_ITER
    assert n_chunks % cpi == 0
    tb = pl.program_id(1)

    @pl.when(tb == 0)
    def _():
        st_ref[...] = jnp.zeros_like(st_ref)

    lbl = lbl_ref[...]
    e = jnp.exp(lbl - jnp.max(lbl, axis=0, keepdims=True))
    lb_all = jnp.sum(e[:layer + 1], axis=0, keepdims=True) / jnp.sum(e, axis=0, keepdims=True)
    nw = nw_ref[...]

    rr = lax.broadcasted_iota(jnp.int32, (chunk, chunk), 0)
    cc = lax.broadcasted_iota(jnp.int32, (chunk, chunk), 1)
    incl = cc <= rr
    rr2 = lax.broadcasted_iota(jnp.int32, (chunk, 2 * chunk), 0)
    cc2 = lax.broadcasted_iota(jnp.int32, (chunk, 2 * chunk), 1) % chunk
    tril2 = jnp.where(cc2 <= rr2, 1.0, 0.0).astype(BF16)

    def gates(c0, h):
        hs = slice(h * d, (h + 1) * d)
        rows = pl.ds(c0, chunk)
        lb = lb_all[:, hs]
        t1 = (0.5 * (1.0 - lb)) * jnp.tanh(f_ref[0, rows, hs].astype(F32))
        log_f = jnp.log(0.5 * (1.0 + lb) + t1)
        k_in = 0.5 * (1.0 - lb) - t1
        hq = q_ref[0, rows, hs].astype(F32)
        qs = hq + hq * jnp.tanh(hq)
        return qs, k_in, log_f

    def finish(c0, h, out):
        hs = slice(h * d, (h + 1) * d)
        rows = pl.ds(c0, chunk)
        hg = g_ref[0, rows, hs].astype(F32)
        gate = hg + hg * jnp.tanh(hg)
        o_ref[0, rows, hs] = (_rms(out, nw) * gate).astype(o_ref.dtype)

    def fixup(c0, u):
        ones8 = jnp.ones((8, d), BF16)
        scores = cum_ref.at[cpi * heads]
        for h in range(heads):
            hs = slice(h * d, (h + 1) * d)
            qs, k_in, _unused = gates(c0, h)
            cum = cum_ref[u * heads + h]
            v = i_ref[0, pl.ds(c0, chunk), hs]

            def row(t, _c):
                sel = rr[:, :1] == t
                q_t = jnp.sum(jnp.where(sel, qs, 0.0), axis=0, keepdims=True)
                cum_t = jnp.sum(jnp.where(sel, cum, 0.0), axis=0, keepdims=True)
                p = q_t * k_in * jnp.exp(jnp.minimum(cum_t - cum, 0.0))
                srow = lax.dot_general(ones8, p.astype(BF16), _NT,
                                       preferred_element_type=F32)
                scores[pl.ds(t, 1), :] = jnp.where(cc[:1, :] <= t, srow[:1, :], 0.0)
                return _c

            lax.fori_loop(0, chunk, row, 0)
            qg = (qs * jnp.exp(cum)).astype(BF16)
            out = (jnp.dot(scores[...].astype(BF16), v, preferred_element_type=F32)
                   + lax.dot_general(qg, st_ref[u, h].astype(BF16), _NT,
                                     preferred_element_type=F32))
            finish(c0, h, out)

    def iter_body(it, _):
        hr = range(heads)
        hsl = [slice(h * d, (h + 1) * d) for h in hr]
        c0 = [pl.multiple_of((it * cpi + u) * chunk, chunk) for u in range(cpi)]
        uh = [(u, h) for u in range(cpi) for h in hr]
        g = {k: gates(c0[k[0]], k[1]) for k in uh}
        cum = {}
        for u, h in uh:
            hi, lo = _split_bf16(g[u, h][2])
            cum[u, h] = jnp.dot(tril2, jnp.concatenate([hi, lo], axis=0),
                                preferred_element_type=F32)
            cum_ref[u * heads + h] = cum[u, h]
        last = {k: cum[k][chunk - 1:chunk, :] for k in uh}
        qg = {k: (g[k][0] * jnp.exp(cum[k])).astype(BF16) for k in uh}
        kd = {k: g[k][1] * jnp.exp(last[k] - cum[k]) for k in uh}
        kgt = {k: (kd[k] * jnp.exp(jnp.minimum(-last[k], HG_SAFE_LOG))).T.astype(BF16)
               for k in uh}
        v = {(u, h): i_ref[0, pl.ds(c0[u], chunk), hsl[h]] for u, h in uh}
        a = {k: jnp.dot(qg[k], kgt[k], preferred_element_type=F32) for k in uh}
        st = [st_ref[0, h] for h in hr]
        inter = {}
        for u in range(cpi):
            for h in hr:
                inter[u, h] = jnp.dot(qg[u, h], st[h].T.astype(BF16),
                                      preferred_element_type=F32)
            for h in hr:
                vt = v[u, h].astype(F32).T.astype(BF16)
                st[h] = (st[h] * jnp.exp(last[u, h])
                         + jnp.dot(vt, kd[u, h].astype(BF16), preferred_element_type=F32))
                st_ref[u + 1, h] = st[h]
        for u, h in uh:
            am = jnp.where(incl, a[u, h], 0.0).astype(BF16)
            finish(c0[u], h, jnp.dot(am, v[u, h], preferred_element_type=F32) + inter[u, h])
        worst = [functools.reduce(jnp.minimum, [jnp.min(last[u, h]) for h in hr])
                 for u in range(cpi)]

        for u in range(cpi):
            pl.when(worst[u] < -HG_SAFE_LOG)(functools.partial(fixup, c0[u], u))

        st_ref[0] = st_ref[cpi]
        return 0

    lax.fori_loop(0, n_chunks // cpi, iter_body, 0)


def _hgrn2(proj3, lb_logits, norm_w, *, col0, width, tb, chunk, layer):
    b, s, _ = proj3.shape
    heads = width // HEAD_DIM
    assert chunk == HEAD_DIM, "score scratch rows reuse the (chunk, HEAD_DIM) cum buffers"
    cb = col0 // width
    kern = functools.partial(_hgrn_kernel, heads=heads, chunk=chunk, layer=layer)

    def col(k):
        return pl.BlockSpec((1, tb, width), lambda bb, t: (bb, t, cb + k))

    return pl.pallas_call(
        kern,
        grid=(b, s // tb),
        in_specs=[col(0), col(1), col(2), col(3),
                  pl.BlockSpec(lb_logits.shape, lambda bb, t: (0, 0)),
                  pl.BlockSpec((1, HEAD_DIM), lambda bb, t: (0, 0))],
        out_specs=pl.BlockSpec((1, tb, width), lambda bb, t: (bb, t, 0)),
        out_shape=jax.ShapeDtypeStruct((b, s, width), BF16),
        scratch_shapes=[pltpu.VMEM((HG_CHUNKS_PER_ITER + 1, heads, HEAD_DIM, HEAD_DIM), F32),
                        pltpu.VMEM((HG_CHUNKS_PER_ITER * heads + 1, chunk, HEAD_DIM), F32)],
        compiler_params=pltpu.CompilerParams(
            dimension_semantics=("parallel", "arbitrary"), vmem_limit_bytes=VMEM_LIMIT_BYTES),
        name="hgrn2",
    )(proj3, proj3, proj3, proj3, lb_logits, norm_w.reshape(1, HEAD_DIM))


def _out_proj_kernel(oa_ref, ob_ref, wa_ref, wb_ref, x_ref, nw_ref, h_ref, m_ref):
    acc = (jnp.dot(oa_ref[...], wa_ref[...], preferred_element_type=F32)
           + jnp.dot(ob_ref[...], wb_ref[...], preferred_element_type=F32))
    h = x_ref[...] + acc
    h_ref[...] = h
    m_ref[...] = _rms(h, nw_ref[...]).astype(m_ref.dtype)


def _out_proj(oa, ob, w_out, x2, norm_w, *, tm):
    t, d = x2.shape
    wa = oa.shape[1]
    wb = ob.shape[1]
    assert wa == wb
    return pl.pallas_call(
        _out_proj_kernel,
        grid=(t // tm,),
        in_specs=[
            pl.BlockSpec((tm, wa), lambda i: (i, 0)),
            pl.BlockSpec((tm, wb), lambda i: (i, 0)),
            pl.BlockSpec((wa, d), lambda i: (0, 0)),
            pl.BlockSpec((wb, d), lambda i: (1, 0)),
            pl.BlockSpec((tm, d), lambda i: (i, 0)),
            pl.BlockSpec((1, d), lambda i: (0, 0)),
        ],
        out_specs=[pl.BlockSpec((tm, d), lambda i: (i, 0)),
                   pl.BlockSpec((tm, d), lambda i: (i, 0))],
        out_shape=[jax.ShapeDtypeStruct((t, d), F32), jax.ShapeDtypeStruct((t, d), BF16)],
        compiler_params=pltpu.CompilerParams(
            dimension_semantics=("parallel",), vmem_limit_bytes=VMEM_LIMIT_BYTES),
        name="out_proj",
    )(oa, ob, w_out, w_out, x2, norm_w.reshape(1, d))


def _mlp_kernel(m_ref, wu_ref, wd_ref, h_hbm, nw_ref, y_hbm, acc_ref, sem):
    i = pl.program_id(0)
    j = pl.program_id(1)
    ni = pl.num_programs(0)
    nj = pl.num_programs(1)
    tm = acc_ref.shape[1]
    s = i % 2

    def h_copy(blk, slot):
        return pltpu.make_async_copy(h_hbm.at[pl.ds(blk * tm, tm)], acc_ref.at[slot],
                                     sem.at[0, slot])

    def y_copy(blk, slot):
        return pltpu.make_async_copy(acc_ref.at[slot], y_hbm.at[pl.ds(blk * tm, tm)],
                                     sem.at[1, slot])

    @pl.when(jnp.logical_and(i == 0, j == 0))
    def _():
        h_copy(0, 0).start()

    @pl.when(j == 0)
    def _():
        h_copy(i, s).wait()

    @pl.when(j == 1)
    def _():
        @pl.when(i >= 1)
        def _():
            y_copy(i - 1, 1 - s).wait()

        @pl.when(i + 1 < ni)
        def _():
            h_copy(i + 1, 1 - s).start()

    r = jnp.maximum(jnp.dot(m_ref[...], wu_ref[...], preferred_element_type=F32), 0.0)
    acc_ref[s] += jnp.dot((r * r).astype(BF16), wd_ref[...], preferred_element_type=F32)

    @pl.when(j == nj - 1)
    def _():
        acc_ref[s] = _rms(acc_ref[s], nw_ref[...])
        y_copy(i, s).start()

        @pl.when(i == ni - 1)
        def _():
            y_copy(i, s).wait()


def _mlp(m, w_up, w_down, h, norm_w, *, tm, tf):
    t, d = m.shape
    f = w_up.shape[1]
    assert f // tf >= 2, "the slot hand-over happens at the second inner step"
    return pl.pallas_call(
        _mlp_kernel,
        grid=(t // tm, f // tf),
        in_specs=[pl.BlockSpec((tm, d), lambda i, j: (i, 0)),
                  pl.BlockSpec((d, tf), lambda i, j: (0, j)),
                  pl.BlockSpec((tf, d), lambda i, j: (j, 0)),
                  pl.BlockSpec(memory_space=pl.ANY),
                  pl.BlockSpec((1, d), lambda i, j: (0, 0))],
        out_specs=pl.BlockSpec(memory_space=pl.ANY),
        out_shape=jax.ShapeDtypeStruct((t, d), F32),
        scratch_shapes=[pltpu.VMEM((2, tm, d), F32), pltpu.SemaphoreType.DMA((2, 2))],
        compiler_params=pltpu.CompilerParams(
            dimension_semantics=("arbitrary", "arbitrary"), vmem_limit_bytes=VMEM_LIMIT_BYTES),
        name="mlp",
    )(m, w_up, w_down, h, norm_w.reshape(1, d))


def _block(x, attn_norm_w, w_in, lb_logits, sb_norm_w, hg_norm_w, w_out,
           mlp_norm_w, w_up, w_down, final_norm_w, *, tiles):
    b, s, d = x.shape
    depth = w_in.shape[0]
    sb_width = d // 2
    hg_width = d // 2
    x2 = x.reshape(b * s, d)
    h = x2
    for layer in range(depth):
        proj = _in_proj(h, attn_norm_w[layer], w_in[layer],
                        group_scales=(HEAD_DIM ** -0.5 * LOG2_E, 1.0, 1.0, 0.5, 0.5, 1.0, 0.5),
                        tm=tiles["proj_tm"], tn=tiles["proj_tn"])
        proj3 = proj.reshape(b, s, -1)
        o_a, w_out_bf, w_up_bf, w_down_bf = _stick_break(
            proj3, sb_norm_w[layer], (w_out[layer], w_up[layer], w_down[layer]),
            sb_width=sb_width, blk=tiles["sb_blk"], nprev=tiles["sb_prev"])
        o_b = _hgrn2(proj3, lb_logits, hg_norm_w[layer], col0=3 * sb_width, width=hg_width,
                     tb=tiles["hg_tb"], chunk=HEAD_DIM, layer=layer)
        h, m = _out_proj(o_a.reshape(b * s, sb_width), o_b.reshape(b * s, hg_width),
                         w_out_bf, h, mlp_norm_w[layer], tm=tiles["out_tm"])
        assert layer == depth - 1, "only the single-layer block is fused end to end"
        h = _mlp(m, w_up_bf, w_down_bf, h, final_norm_w,
                 tm=tiles["mlp_tm"], tf=tiles["mlp_tf"])
    return h.reshape(b, s, d)


_TILES = dict(proj_tm=2048, proj_tn=512, sb_blk=256, sb_prev=1, hg_tb=1024,
              out_tm=512, mlp_tm=1024, mlp_tf=1024)


def kernel(x, attn_norm_w, w_in, lb_logits, sb_norm_w, hg_norm_w, w_out,
           mlp_norm_w, w_up, w_down, final_norm_w):
    return _block(x, attn_norm_w, w_in, lb_logits, sb_norm_w, hg_norm_w, w_out,
                  mlp_norm_w, w_up, w_down, final_norm_w, tiles=_TILES)
```

```python
import functools

import jax
import jax.numpy as jnp
from jax import lax
from jax.experimental import pallas as pl
from jax.experimental.pallas import tpu as pltpu

F32 = jnp.float32
BF16 = jnp.bfloat16

HEAD_DIM = 128
NORM_EPS = 1e-5
VMEM_LIMIT_BYTES = 56 * 1024 * 1024

LANES = 128
LOG2_E = 1.4426950408889634
SB_ZERO_LOG2 = -152.0
SB_NO_BLOCK_LOG2 = -1e30
HG_SAFE_LOG = 80.0
HG_CHUNKS_PER_ITER = 2

_NT = (((1,), (1,)), ((), ()))


def _rms(x, w):
    return x * lax.rsqrt(jnp.mean(x * x, axis=-1, keepdims=True) + NORM_EPS) * w


def _split_bf16(x):
    hi = x.astype(BF16)
    lo = (x - hi.astype(F32)).astype(BF16)
    return hi, lo


def _in_proj_kernel(x_hbm, nw_ref, w_ref, o_ref, xbuf, u_ref, sem, *, col_scales):
    i = pl.program_id(0)
    j = pl.program_id(1)
    tm = xbuf.shape[0]

    def x_copy(blk):
        return pltpu.make_async_copy(x_hbm.at[pl.ds(blk * tm, tm)], xbuf, sem.at[0])

    @pl.when(j == 0)
    def _():
        @pl.when(i == 0)
        def _():
            x_copy(0).start()

        x_copy(i).wait()
        u_ref[...] = _rms(xbuf[...], nw_ref[...]).astype(BF16)

        @pl.when(i + 1 < pl.num_programs(0))
        def _():
            x_copy(i + 1).start()

    acc = jnp.dot(u_ref[...], w_ref[...].astype(BF16), preferred_element_type=F32)
    scale = jnp.float32(col_scales[0])
    for jj in range(1, len(col_scales)):
        scale = jnp.where(j == jj, jnp.float32(col_scales[jj]), scale)
    o_ref[...] = (acc * scale).astype(o_ref.dtype)


def _in_proj(x2, norm_w, w, *, group_scales, tm, tn):
    t, d = x2.shape
    n = w.shape[1]
    per_group = n // len(group_scales) // tn
    assert per_group * tn * len(group_scales) == n
    kern = functools.partial(
        _in_proj_kernel, col_scales=tuple(s for s in group_scales for _ in range(per_group)))
    return pl.pallas_call(
        kern,
        grid=(t // tm, n // tn),
        in_specs=[
            pl.BlockSpec(memory_space=pl.ANY),
            pl.BlockSpec((1, d), lambda i, j: (0, 0)),
            pl.BlockSpec((d, tn), lambda i, j: (0, j)),
        ],
        out_specs=pl.BlockSpec((tm, tn), lambda i, j: (i, j)),
        out_shape=jax.ShapeDtypeStruct((t, n), BF16),
        scratch_shapes=[pltpu.VMEM((tm, d), F32), pltpu.VMEM((tm, d), BF16),
                        pltpu.SemaphoreType.DMA((1,))],
        compiler_params=pltpu.CompilerParams(
            dimension_semantics=("arbitrary", "arbitrary"), vmem_limit_bytes=VMEM_LIMIT_BYTES),
        name="in_proj",
    )(x2, norm_w.reshape(1, d), w)


def _sb_kernel(*refs, heads, blk, nprev, ncast, k_col, v_col):
    q_ref = refs[0]
    k_refs = refs[1:2 + nprev]
    v_refs = refs[2 + nprev:3 + 2 * nprev]
    rest = refs[3 + 2 * nprev:]
    proj_hbm, nw_ref = rest[:2]
    cast_in = rest[2:2 + ncast]
    o_ref = rest[2 + ncast]
    cast_out = rest[3 + ncast:3 + 2 * ncast]
    acc_ref, carry_ref, kbuf, vbuf, sem = rest[3 + 2 * ncast:]
    b = pl.program_id(0)
    i = pl.program_id(1)
    d = HEAD_DIM
    sub = LANES
    nsub = blk // sub
    width = heads * d

    for src, dst in zip(cast_in, cast_out):
        dst[...] = src[...].astype(BF16)

    r = lax.broadcasted_iota(jnp.int32, (2 * sub, 2 * sub), 0) % sub
    c = lax.broadcasted_iota(jnp.int32, (2 * sub, 2 * sub), 1)
    wmat = jnp.where((c >= sub) | (r > c), -1.0, 0.0).astype(BF16)

    hsl = [slice(h * d, (h + 1) * d) for h in range(heads)]

    def diag_only(x, mask, fill):
        n = mask.shape[1]
        return jnp.concatenate([x[:, :-n], jnp.where(mask, x[:, -n:], fill)], axis=1)

    def scan(z, diag_mask, carry, enter_bias, v_of, valid=None):
        nc = z.shape[1] // sub
        sp = jnp.maximum(z, 0.0) + jnp.log2(1.0 + jnp.exp2(-jnp.abs(z)))
        log_beta = z - sp
        if diag_mask is not None:
            sp = diag_only(sp, diag_mask, 0.0)
        if valid is not None:
            sp = jnp.where(valid, sp, 0.0)
        hi, lo = _split_bf16(sp)
        after = [None] * nc
        for cb in reversed(range(nc)):
            cs = slice(cb * sub, (cb + 1) * sub)
            sums = jnp.dot(jnp.concatenate([hi[:, cs], lo[:, cs]], axis=1), wmat,
                           preferred_element_type=F32)
            after[cb] = sums[:, :sub] if carry is None else sums[:, :sub] + carry
            carry = sums[:, sub:] if carry is None else carry + sums[:, sub:]
            if cb > 0 and enter_bias[cb - 1] is not None:
                carry = carry + enter_bias[cb - 1]
        w = jnp.exp2(log_beta + jnp.concatenate(after, axis=1))
        if diag_mask is not None:
            w = diag_only(w, diag_mask, 0.0)
        if valid is not None:
            w = jnp.where(valid, w, 0.0)
        wb = w.astype(BF16)
        pv = {(h, g): jnp.dot(wb[h * blk + g * sub:h * blk + (g + 1) * sub], v_of(h, g),
                              preferred_element_type=F32)
              for h in range(heads) for g in range(nsub)}
        return pv, carry

    def logits(k_srcs):
        return [jnp.concatenate(
            [lax.dot_general(q_ref[0, :, hs], k_src[:, hs], _NT, preferred_element_type=F32)
             for k_src in k_srcs], axis=1) for hs in hsl]

    ncols = (nprev + 1) * nsub
    win = ncols - nsub + 1
    z_h = logits([k_refs[k].at[0] for k in reversed(range(nprev + 1))])
    z_win = jnp.concatenate(
        [jnp.concatenate([z_h[h][g * sub:(g + 1) * sub, g * sub:(g + win) * sub]
                          for g in range(nsub)], axis=0) for h in range(heads)], axis=0)
    missing = [jnp.where(i >= k, 0.0, SB_NO_BLOCK_LOG2).astype(F32) for k in range(1, nprev + 1)]

    def bias_entering(p):
        per_group = []
        for g in range(nsub):
            c = g + p
            back = nprev - c // nsub
            newest = (c + 1) % nsub == 0
            per_group.append(missing[back - 1] if back >= 1 and newest else None)
        if all(b is None for b in per_group):
            return None
        rows = [jnp.full((sub, sub), 0.0 if b is None else b, F32) for b in per_group]
        return jnp.concatenate(rows * heads, axis=0)

    v_cat = [jnp.concatenate([v_refs[k][0, :, hs] for k in reversed(range(nprev + 1))], axis=0)
             for hs in hsl]
    tri_sub = (lax.broadcasted_iota(jnp.int32, (sub, sub), 1)
               < lax.broadcasted_iota(jnp.int32, (sub, sub), 0))
    pv, carry = scan(z_win, jnp.concatenate([tri_sub] * (heads * nsub), axis=0), None,
                     [bias_entering(p) for p in range(win - 1)],
                     lambda h, g: v_cat[h][g * sub:(g + win) * sub])
    for (h, g), val in pv.items():
        acc_ref[h, g * sub:(g + 1) * sub, :] = val
    carry_ref[...] = carry

    rowg = (lax.broadcasted_iota(jnp.int32, (heads * blk, blk), 0) % blk) // sub
    colc = lax.broadcasted_iota(jnp.int32, (heads * blk, blk), 1) // sub
    left_out = colc < rowg

    def tiles(k_src, v_src, redo):
        z = jnp.concatenate(logits([k_src]), axis=0)
        pv, carry = scan(z, None, carry_ref[...], [None] * (nsub - 1),
                         lambda h, g: v_src[:, hsl[h]],
                         valid=jnp.logical_or(jnp.logical_not(redo), left_out))
        for (h, g), val in pv.items():
            acc_ref[h, g * sub:(g + 1) * sub, :] += val
        carry_ref[...] = carry

    def fetch(j):
        rows = pl.ds(pl.multiple_of(j * blk, blk), blk)
        return (pltpu.make_async_copy(proj_hbm.at[b, rows, pl.ds(k_col, width)], kbuf, sem.at[0]),
                pltpu.make_async_copy(proj_hbm.at[b, rows, pl.ds(v_col, width)], vbuf, sem.at[1]))

    def cond(state):
        j, live = state
        return jnp.logical_and(j >= 0, live > SB_ZERO_LOG2)

    def body(state):
        j, _ = state
        for cp in fetch(j):
            cp.start()
        for cp in fetch(j):
            cp.wait()
        tiles(kbuf, vbuf, j == i - nprev)
        return j - 1, jnp.max(carry_ref[...])

    lax.while_loop(cond, body, (i - nprev, jnp.max(carry_ref[...])))

    nw = nw_ref[...]
    for h in range(heads):
        o_ref[0, :, h * d:(h + 1) * d] = _rms(acc_ref[h], nw).astype(o_ref.dtype)


def _stick_break(proj3, norm_w, cast_weights, *, sb_width, blk, nprev):
    b, s, _ = proj3.shape
    heads = sb_width // HEAD_DIM
    nq = s // blk
    steps = b * nq
    for w in cast_weights:
        assert w.shape[0] % (steps * 16) == 0, "a bf16 slab needs a multiple of 16 rows"
    cast_specs = [pl.BlockSpec((w.shape[0] // steps, w.shape[1]), lambda bb, i: (bb * nq + i, 0))
                  for w in cast_weights]
    kern = functools.partial(_sb_kernel, heads=heads, blk=blk, nprev=nprev,
                             ncast=len(cast_weights), k_col=sb_width, v_col=2 * sb_width)

    def back(k, col):
        return pl.BlockSpec((1, blk, sb_width), lambda bb, i: (bb, jnp.maximum(i - k, 0), col))

    return pl.pallas_call(
        kern,
        grid=(b, nq),
        in_specs=[
            back(0, 0),
            *[back(k, 1) for k in range(nprev + 1)],
            *[back(k, 2) for k in range(nprev + 1)],
            pl.BlockSpec(memory_space=pl.ANY),
            pl.BlockSpec((1, HEAD_DIM), lambda bb, i: (0, 0)),
            *cast_specs,
        ],
        out_specs=[pl.BlockSpec((1, blk, sb_width), lambda bb, i: (bb, i, 0)), *cast_specs],
        out_shape=[jax.ShapeDtypeStruct((b, s, sb_width), BF16),
                   *[jax.ShapeDtypeStruct(w.shape, BF16) for w in cast_weights]],
        scratch_shapes=[pltpu.VMEM((heads, blk, HEAD_DIM), F32),
                        pltpu.VMEM((heads * blk, LANES), F32),
                        pltpu.VMEM((blk, sb_width), BF16),
                        pltpu.VMEM((blk, sb_width), BF16),
                        pltpu.SemaphoreType.DMA((2,))],
        compiler_params=pltpu.CompilerParams(
            dimension_semantics=("parallel", "arbitrary"), vmem_limit_bytes=VMEM_LIMIT_BYTES),
        name="stick_break",
    )(*([proj3] * (4 + 2 * nprev)), norm_w.reshape(1, HEAD_DIM), *cast_weights)


def _hgrn_kernel(q_ref, f_ref, i_ref, g_ref, lbl_ref, nw_ref, o_ref,
                 st_ref, cum_ref, *, heads, chunk, layer):
    d = HEAD_DIM
    n_chunks = q_ref.shape[1] // chunk
    cpi = HG_CHUNKS_PER_ITER
    assert n_chunks % cpi == 0
    tb = pl.program_id(1)

    @pl.when(tb == 0)
    def _():
        st_ref[...] = jnp.zeros_like(st_ref)

    lbl = lbl_ref[...]
    e = jnp.exp(lbl - jnp.max(lbl, axis=0, keepdims=True))
    lb_all = jnp.sum(e[:layer + 1], axis=0, keepdims=True) / jnp.sum(e, axis=0, keepdims=True)
    nw = nw_ref[...]

    rr = lax.broadcasted_iota(jnp.int32, (chunk, chunk), 0)
    cc = lax.broadcasted_iota(jnp.int32, (chunk, chunk), 1)
    incl = cc <= rr
    rr2 = lax.broadcasted_iota(jnp.int32, (chunk, 2 * chunk), 0)
    cc2 = lax.broadcasted_iota(jnp.int32, (chunk, 2 * chunk), 1) % chunk
    tril2 = jnp.where(cc2 <= rr2, 1.0, 0.0).astype(BF16)

    def gates(c0, h):
        hs = slice(h * d, (h + 1) * d)
        rows = pl.ds(c0, chunk)
        lb = lb_all[:, hs]
        t1 = (0.5 * (1.0 - lb)) * jnp.tanh(f_ref[0, rows, hs].astype(F32))
        log_f = jnp.log(0.5 * (1.0 + lb) + t1)
        k_in = 0.5 * (1.0 - lb) - t1
        hq = q_ref[0, rows, hs].astype(F32)
        qs = hq + hq * jnp.tanh(hq)
        return qs, k_in, log_f

    def finish(c0, h, out):
        hs = slice(h * d, (h + 1) * d)
        rows = pl.ds(c0, chunk)
        hg = g_ref[0, rows, hs].astype(F32)
        gate = hg + hg * jnp.tanh(hg)
        o_ref[0, rows, hs] = (_rms(out, nw) * gate).astype(o_ref.dtype)

    def fixup(c0, u):
        ones8 = jnp.ones((8, d), BF16)
        scores = cum_ref.at[cpi * heads]
        for h in range(heads):
            hs = slice(h * d, (h + 1) * d)
            qs, k_in, _unused = gates(c0, h)
            cum = cum_ref[u * heads + h]
            v = i_ref[0, pl.ds(c0, chunk), hs]

            def row(t, _c):
                sel = rr[:, :1] == t
                q_t = jnp.sum(jnp.where(sel, qs, 0.0), axis=0, keepdims=True)
                cum_t = jnp.sum(jnp.where(sel, cum, 0.0), axis=0, keepdims=True)
                p = q_t * k_in * jnp.exp(jnp.minimum(cum_t - cum, 0.0))
                srow = lax.dot_general(ones8, p.astype(BF16), _NT,
                                       preferred_element_type=F32)
                scores[pl.ds(t, 1), :] = jnp.where(cc[:1, :] <= t, srow[:1, :], 0.0)
                return _c

            lax.fori_loop(0, chunk, row, 0)
            qg = (qs * jnp.exp(cum)).astype(BF16)
            out = (jnp.dot(scores[...].astype(BF16), v, preferred_element_type=F32)
                   + lax.dot_general(qg, st_ref[u, h].astype(BF16), _NT,
                                     preferred_element_type=F32))
            finish(c0, h, out)

    def iter_body(it, _):
        hr = range(heads)
        hsl = [slice(h * d, (h + 1) * d) for h in hr]
        c0 = [pl.multiple_of((it * cpi + u) * chunk, chunk) for u in range(cpi)]
        uh = [(u, h) for u in range(cpi) for h in hr]
        g = {k: gates(c0[k[0]], k[1]) for k in uh}
        cum = {}
        for u, h in uh:
            hi, lo = _split_bf16(g[u, h][2])
            cum[u, h] = jnp.dot(tril2, jnp.concatenate([hi, lo], axis=0),
                                preferred_element_type=F32)
            cum_ref[u * heads + h] = cum[u, h]
        last = {k: cum[k][chunk - 1:chunk, :] for k in uh}
        qg = {k: (g[k][0] * jnp.exp(cum[k])).astype(BF16) for k in uh}
        kd = {k: g[k][1] * jnp.exp(last[k] - cum[k]) for k in uh}
        kgt = {k: (kd[k] * jnp.exp(jnp.minimum(-last[k], HG_SAFE_LOG))).T.astype(BF16)
               for k in uh}
        v = {(u, h): i_ref[0, pl.ds(c0[u], chunk), hsl[h]] for u, h in uh}
        a = {k: jnp.dot(qg[k], kgt[k], preferred_element_type=F32) for k in uh}
        st = [st_ref[0, h] for h in hr]
        inter = {}
        for u in range(cpi):
            for h in hr:
                inter[u, h] = jnp.dot(qg[u, h], st[h].T.astype(BF16),
                                      preferred_element_type=F32)
            for h in hr:
                vt = v[u, h].astype(F32).T.astype(BF16)
                st[h] = (st[h] * jnp.exp(last[u, h])
                         + jnp.dot(vt, kd[u, h].astype(BF16), preferred_element_type=F32))
                st_ref[u + 1, h] = st[h]
        for u, h in uh:
            am = jnp.where(incl, a[u, h], 0.0).astype(BF16)
            finish(c0[u], h, jnp.dot(am, v[u, h], preferred_element_type=F32) + inter[u, h])
        worst = [functools.reduce(jnp.minimum, [jnp.min(last[u, h]) for h in hr])
                 for u in range(cpi)]

        for u in range(cpi):
            pl.when(worst[u] < -HG_SAFE_LOG)(functools.partial(fixup, c0[u], u))

        st_ref[0] = st_ref[cpi]
        return 0

    lax.fori_loop(0, n_chunks // cpi, iter_body, 0)


def _hgrn2(proj3, lb_logits, norm_w, *, col0, width, tb, chunk, layer):
    b, s, _ = proj3.shape
    heads = width // HEAD_DIM
    assert chunk == HEAD_DIM, "score scratch rows reuse the (chunk, HEAD_DIM) cum buffers"
    cb = col0 // width
    kern = functools.partial(_hgrn_kernel, heads=heads, chunk=chunk, layer=layer)

    def col(k):
        return pl.BlockSpec((1, tb, width), lambda bb, t: (bb, t, cb + k))

    return pl.pallas_call(
        kern,
        grid=(b, s // tb),
        in_specs=[col(0), col(1), col(2), col(3),
                  pl.BlockSpec(lb_logits.shape, lambda bb, t: (0, 0)),
                  pl.BlockSpec((1, HEAD_DIM), lambda bb, t: (0, 0))],
        out_specs=pl.BlockSpec((1, tb, width), lambda bb, t: (bb, t, 0)),
        out_shape=jax.ShapeDtypeStruct((b, s, width), BF16),
        scratch_shapes=[pltpu.VMEM((HG_CHUNKS_PER_ITER + 1, heads, HEAD_DIM, HEAD_DIM), F32),
                        pltpu.VMEM((HG_CHUNKS_PER_ITER * heads + 1, chunk, HEAD_DIM), F32)],
        compiler_params=pltpu.CompilerParams(
            dimension_semantics=("parallel", "arbitrary"), vmem_limit_bytes=VMEM_LIMIT_BYTES),
        name="hgrn2",
    )(proj3, proj3, proj3, proj3, lb_logits, norm_w.reshape(1, HEAD_DIM))


def _out_proj_kernel(oa_ref, ob_ref, wa_ref, wb_ref, x_ref, nw_ref, h_ref, m_ref):
    acc = (jnp.dot(oa_ref[...], wa_ref[...], preferred_element_type=F32)
           + jnp.dot(ob_ref[...], wb_ref[...], preferred_element_type=F32))
    h = x_ref[...] + acc
    h_ref[...] = h
    m_ref[...] = _rms(h, nw_ref[...]).astype(m_ref.dtype)


def _out_proj(oa, ob, w_out, x2, norm_w, *, tm):
    t, d = x2.shape
    wa = oa.shape[1]
    wb = ob.shape[1]
    assert wa == wb
    return pl.pallas_call(
        _out_proj_kernel,
        grid=(t // tm,),
        in_specs=[
            pl.BlockSpec((tm, wa), lambda i: (i, 0)),
            pl.BlockSpec((tm, wb), lambda i: (i, 0)),
            pl.BlockSpec((wa, d), lambda i: (0, 0)),
            pl.BlockSpec((wb, d), lambda i: (1, 0)),
            pl.BlockSpec((tm, d), lambda i: (i, 0)),
            pl.BlockSpec((1, d), lambda i: (0, 0)),
        ],
        out_specs=[pl.BlockSpec((tm, d), lambda i: (i, 0)),
                   pl.BlockSpec((tm, d), lambda i: (i, 0))],
        out_shape=[jax.ShapeDtypeStruct((t, d), F32), jax.ShapeDtypeStruct((t, d), BF16)],
        compiler_params=pltpu.CompilerParams(
            dimension_semantics=("parallel",), vmem_limit_bytes=VMEM_LIMIT_BYTES),
        name="out_proj",
    )(oa, ob, w_out, w_out, x2, norm_w.reshape(1, d))


def _mlp_kernel(m_ref, wu_ref, wd_ref, h_hbm, nw_ref, y_hbm, acc_ref, sem):
    i = pl.program_id(0)
    j = pl.program_id(1)
    ni = pl.num_programs(0)
    nj = pl.num_programs(1)
    tm = acc_ref.shape[1]
    s = i % 2

    def h_copy(blk, slot):
        return pltpu.make_async_copy(h_hbm.at[pl.ds(blk * tm, tm)], acc_ref.at[slot],
                                     sem.at[0, slot])

    def y_copy(blk, slot):
        return pltpu.make_async_copy(acc_ref.at[slot], y_hbm.at[pl.ds(blk * tm, tm)],
                                     sem.at[1, slot])

    @pl.when(jnp.logical_and(i == 0, j == 0))
    def _():
        h_copy(0, 0).start()

    @pl.when(j == 0)
    def _():
        h_copy(i, s).wait()

    @pl.when(j == 1)
    def _():
        @pl.when(i >= 1)
        def _():
            y_copy(i - 1, 1 - s).wait()

        @pl.when(i + 1 < ni)
        def _():
            h_copy(i + 1, 1 - s).start()

    r = jnp.maximum(jnp.dot(m_ref[...], wu_ref[...], preferred_element_type=F32), 0.0)
    acc_ref[s] += jnp.dot((r * r).astype(BF16), wd_ref[...], preferred_element_type=F32)

    @pl.when(j == nj - 1)
    def _():
        acc_ref[s] = _rms(acc_ref[s], nw_ref[...])
        y_copy(i, s).start()

        @pl.when(i == ni - 1)
        def _():
            y_copy(i, s).wait()


def _mlp(m, w_up, w_down, h, norm_w, *, tm, tf):
    t, d = m.shape
    f = w_up.shape[1]
    assert f // tf >= 2, "the slot hand-over happens at the second inner step"
    return pl.pallas_call(
        _mlp_kernel,
        grid=(t // tm, f // tf),
        in_specs=[pl.BlockSpec((tm, d), lambda i, j: (i, 0)),
                  pl.BlockSpec((d, tf), lambda i, j: (0, j)),
                  pl.BlockSpec((tf, d), lambda i, j: (j, 0)),
                  pl.BlockSpec(memory_space=pl.ANY),
                  pl.BlockSpec((1, d), lambda i, j: (0, 0))],
        out_specs=pl.BlockSpec(memory_space=pl.ANY),
        out_shape=jax.ShapeDtypeStruct((t, d), F32),
        scratch_shapes=[pltpu.VMEM((2, tm, d), F32), pltpu.SemaphoreType.DMA((2, 2))],
        compiler_params=pltpu.CompilerParams(
            dimension_semantics=("arbitrary", "arbitrary"), vmem_limit_bytes=VMEM_LIMIT_BYTES),
        name="mlp",
    )(m, w_up, w_down, h, norm_w.reshape(1, d))


def _block(x, attn_norm_w, w_in, lb_logits, sb_norm_w, hg_norm_w, w_out,
           mlp_norm_w, w_up, w_down, final_norm_w, *, tiles):
    b, s, d = x.shape
    depth = w_in.shape[0]
    sb_width = d // 2
    hg_width = d // 2
    x2 = x.reshape(b * s, d)
    h = x2
    for layer in range(depth):
        proj = _in_proj(h, attn_norm_w[layer], w_in[layer],
                        group_scales=(HEAD_DIM ** -0.5 * LOG2_E, 1.0, 1.0, 0.5, 0.5, 1.0, 0.5),
                        tm=tiles["proj_tm"], tn=tiles["proj_tn"])
        proj3 = proj.reshape(b, s, -1)
        o_a, w_out_bf, w_up_bf, w_down_bf = _stick_break(
            proj3, sb_norm_w[layer], (w_out[layer], w_up[layer], w_down[layer]),
            sb_width=sb_width, blk=tiles["sb_blk"], nprev=tiles["sb_prev"])
        o_b = _hgrn2(proj3, lb_logits, hg_norm_w[layer], col0=3 * sb_width, width=hg_width,
                     tb=tiles["hg_tb"], chunk=HEAD_DIM, layer=layer)
        h, m = _out_proj(o_a.reshape(b * s, sb_width), o_b.reshape(b * s, hg_width),
                         w_out_bf, h, mlp_norm_w[layer], tm=tiles["out_tm"])
        assert layer == depth - 1, "only the single-layer block is fused end to end"
        h = _mlp(m, w_up_bf, w_down_bf, h, final_norm_w,
                 tm=tiles["mlp_tm"], tf=tiles["mlp_tf"])
    return h.reshape(b, s, d)


_TILES = dict(proj_tm=2048, proj_tn=512, sb_blk=256, sb_prev=1, hg_tb=1024,
              out_tm=512, mlp_tm=1024, mlp_tf=1024)


def kernel(x, attn_norm_w, w_in, lb_logits, sb_norm_w, hg_norm_w, w_out,
           mlp_norm_w, w_up, w_down, final_norm_w):
    return _block(x, attn_norm_w, w_in, lb_logits, sb_norm_w, hg_norm_w, w_out,
                  mlp_norm_w, w_up, w_down, final_norm_w, tiles=_TILES)
```

```python
import functools

import jax
import jax.numpy as jnp
from jax import lax
from jax.experimental import pallas as pl
from jax.experimental.pallas import tpu as pltpu

F32 = jnp.float32
BF16 = jnp.bfloat16

HEAD_DIM = 128
NORM_EPS = 1e-5
VMEM_LIMIT_BYTES = 56 * 1024 * 1024

LANES = 128
LOG2_E = 1.4426950408889634
SB_ZERO_LOG2 = -152.0
SB_NO_BLOCK_LOG2 = -1e30
HG_SAFE_LOG = 80.0
HG_CHUNKS_PER_STEP = 2

_NT = (((1,), (1,)), ((), ()))


def _rms(x, w):
    return x * lax.rsqrt(jnp.mean(x * x, axis=-1, keepdims=True) + NORM_EPS) * w


def _split_bf16(x):
    hi = x.astype(BF16)
    lo = (x - hi.astype(F32)).astype(BF16)
    return hi, lo


def _in_proj_kernel(x_hbm, nw_ref, w_ref, o_ref, xbuf, u_ref, sem, *, col_scales):
    i = pl.program_id(0)
    j = pl.program_id(1)
    tm = xbuf.shape[0]

    def x_copy(blk):
        return pltpu.make_async_copy(x_hbm.at[pl.ds(blk * tm, tm)], xbuf, sem.at[0])

    @pl.when(j == 0)
    def _():
        @pl.when(i == 0)
        def _():
            x_copy(0).start()

        x_copy(i).wait()
        u_ref[...] = _rms(xbuf[...], nw_ref[...]).astype(BF16)

        @pl.when(i + 1 < pl.num_programs(0))
        def _():
            x_copy(i + 1).start()

    acc = jnp.dot(u_ref[...], w_ref[...].astype(BF16), preferred_element_type=F32)
    scale = jnp.float32(col_scales[0])
    for jj in range(1, len(col_scales)):
        scale = jnp.where(j == jj, jnp.float32(col_scales[jj]), scale)
    o_ref[...] = (acc * scale).astype(o_ref.dtype)


def _in_proj(x2, norm_w, w, *, group_scales, tm, tn):
    t, d = x2.shape
    n = w.shape[1]
    per_group = n // len(group_scales) // tn
    assert per_group * tn * len(group_scales) == n
    kern = functools.partial(
        _in_proj_kernel, col_scales=tuple(s for s in group_scales for _ in range(per_group)))
    return pl.pallas_call(
        kern,
        grid=(t // tm, n // tn),
        in_specs=[
            pl.BlockSpec(memory_space=pl.ANY),
            pl.BlockSpec((1, d), lambda i, j: (0, 0)),
            pl.BlockSpec((d, tn), lambda i, j: (0, j)),
        ],
        out_specs=pl.BlockSpec((tm, tn), lambda i, j: (i, j)),
        out_shape=jax.ShapeDtypeStruct((t, n), BF16),
        scratch_shapes=[pltpu.VMEM((tm, d), F32), pltpu.VMEM((tm, d), BF16),
                        pltpu.SemaphoreType.DMA((1,))],
        compiler_params=pltpu.CompilerParams(
            dimension_semantics=("arbitrary", "arbitrary"), vmem_limit_bytes=VMEM_LIMIT_BYTES),
        name="in_proj",
    )(x2, norm_w.reshape(1, d), w)


def _sb_body(q_ref, k_refs, v_refs, proj_hbm, nw_ref, cast_in, o_ref, cast_out,
             acc_ref, carry_ref, kbuf, vbuf, sem, *, heads, blk, nprev, k_col, v_col):
    b = pl.program_id(0)
    i = pl.program_id(1)
    d = HEAD_DIM
    sub = LANES
    nsub = blk // sub
    width = heads * d

    for src, dst in zip(cast_in, cast_out):
        dst[...] = src[...].astype(BF16)

    r = lax.broadcasted_iota(jnp.int32, (2 * sub, 2 * sub), 0) % sub
    c = lax.broadcasted_iota(jnp.int32, (2 * sub, 2 * sub), 1)
    wmat = jnp.where((c >= sub) | (r > c), -1.0, 0.0).astype(BF16)

    hsl = [slice(h * d, (h + 1) * d) for h in range(heads)]

    def diag_only(x, mask, fill):
        n = mask.shape[1]
        return jnp.concatenate([x[:, :-n], jnp.where(mask, x[:, -n:], fill)], axis=1)

    def scan(z, diag_mask, carry, enter_bias, v_of, valid=None):
        nc = z.shape[1] // sub
        sp = jnp.maximum(z, 0.0) + jnp.log2(1.0 + jnp.exp2(-jnp.abs(z)))
        log_beta = z - sp
        if diag_mask is not None:
            sp = diag_only(sp, diag_mask, 0.0)
        if valid is not None:
            sp = jnp.where(valid, sp, 0.0)
        hi, lo = _split_bf16(sp)
        after = [None] * nc
        for cb in reversed(range(nc)):
            cs = slice(cb * sub, (cb + 1) * sub)
            sums = jnp.dot(jnp.concatenate([hi[:, cs], lo[:, cs]], axis=1), wmat,
                           preferred_element_type=F32)
            after[cb] = sums[:, :sub] if carry is None else sums[:, :sub] + carry
            carry = sums[:, sub:] if carry is None else carry + sums[:, sub:]
            if cb > 0 and enter_bias[cb - 1] is not None:
                carry = carry + enter_bias[cb - 1]
        w = jnp.exp2(log_beta + jnp.concatenate(after, axis=1))
        if diag_mask is not None:
            w = diag_only(w, diag_mask, 0.0)
        if valid is not None:
            w = jnp.where(valid, w, 0.0)
        wb = w.astype(BF16)
        pv = {(h, g): jnp.dot(wb[h * blk + g * sub:h * blk + (g + 1) * sub], v_of(h, g),
                              preferred_element_type=F32)
              for h in range(heads) for g in range(nsub)}
        return pv, carry

    def logits(k_srcs):
        return [jnp.concatenate(
            [lax.dot_general(q_ref[0, :, hs], k_src[:, hs], _NT, preferred_element_type=F32)
             for k_src in k_srcs], axis=1) for hs in hsl]

    ncols = (nprev + 1) * nsub
    win = ncols - nsub + 1
    z_h = logits([k_refs[k].at[0] for k in reversed(range(nprev + 1))])
    z_win = jnp.concatenate(
        [jnp.concatenate([z_h[h][g * sub:(g + 1) * sub, g * sub:(g + win) * sub]
                          for g in range(nsub)], axis=0) for h in range(heads)], axis=0)
    missing = [jnp.where(i >= k, 0.0, SB_NO_BLOCK_LOG2).astype(F32) for k in range(1, nprev + 1)]

    def bias_entering(p):
        per_group = []
        for g in range(nsub):
            c = g + p
            back = nprev - c // nsub
            newest = (c + 1) % nsub == 0
            per_group.append(missing[back - 1] if back >= 1 and newest else None)
        if all(b is None for b in per_group):
            return None
        rows = [jnp.full((sub, sub), 0.0 if b is None else b, F32) for b in per_group]
        return jnp.concatenate(rows * heads, axis=0)

    v_cat = [jnp.concatenate([v_refs[k][0, :, hs] for k in reversed(range(nprev + 1))], axis=0)
             for hs in hsl]
    tri_sub = (lax.broadcasted_iota(jnp.int32, (sub, sub), 1)
               < lax.broadcasted_iota(jnp.int32, (sub, sub), 0))
    pv, carry = scan(z_win, jnp.concatenate([tri_sub] * (heads * nsub), axis=0), None,
                     [bias_entering(p) for p in range(win - 1)],
                     lambda h, g: v_cat[h][g * sub:(g + win) * sub])
    for (h, g), val in pv.items():
        acc_ref[h, g * sub:(g + 1) * sub, :] = val
    carry_ref[...] = carry
    yield

    rowg = (lax.broadcasted_iota(jnp.int32, (heads * blk, blk), 0) % blk) // sub
    colc = lax.broadcasted_iota(jnp.int32, (heads * blk, blk), 1) // sub
    left_out = colc < rowg

    def tiles(k_src, v_src, redo):
        z = jnp.concatenate(logits([k_src]), axis=0)
        pv, carry = scan(z, None, carry_ref[...], [None] * (nsub - 1),
                         lambda h, g: v_src[:, hsl[h]],
                         valid=jnp.logical_or(jnp.logical_not(redo), left_out))
        for (h, g), val in pv.items():
            acc_ref[h, g * sub:(g + 1) * sub, :] += val
        carry_ref[...] = carry

    def fetch(j):
        rows = pl.ds(pl.multiple_of(j * blk, blk), blk)
        return (pltpu.make_async_copy(proj_hbm.at[b, rows, pl.ds(k_col, width)], kbuf, sem.at[0]),
                pltpu.make_async_copy(proj_hbm.at[b, rows, pl.ds(v_col, width)], vbuf, sem.at[1]))

    def cond(state):
        j, live = state
        return jnp.logical_and(j >= 0, live > SB_ZERO_LOG2)

    def body(state):
        j, _ = state
        for cp in fetch(j):
            cp.start()
        for cp in fetch(j):
            cp.wait()
        tiles(kbuf, vbuf, j == i - nprev)
        return j - 1, jnp.max(carry_ref[...])

    lax.while_loop(cond, body, (i - nprev, jnp.max(carry_ref[...])))

    nw = nw_ref[...]
    for h in range(heads):
        o_ref[0, :, h * d:(h + 1) * d] = _rms(acc_ref[h], nw).astype(o_ref.dtype)


def _hg_body(q_ref, f_ref, i_ref, g_ref, lbl_ref, nw_ref, o_ref,
             st_ref, cum_ref, *, heads, chunk, layer):
    d = HEAD_DIM
    cpi = HG_CHUNKS_PER_STEP
    assert q_ref.shape[1] == cpi * chunk
    tb = pl.program_id(1)

    @pl.when(tb == 0)
    def _():
        st_ref[...] = jnp.zeros_like(st_ref)

    lbl = lbl_ref[...]
    e = jnp.exp(lbl - jnp.max(lbl, axis=0, keepdims=True))
    lb_all = jnp.sum(e[:layer + 1], axis=0, keepdims=True) / jnp.sum(e, axis=0, keepdims=True)
    nw = nw_ref[...]

    rr = lax.broadcasted_iota(jnp.int32, (chunk, chunk), 0)
    cc = lax.broadcasted_iota(jnp.int32, (chunk, chunk), 1)
    incl = cc <= rr
    rr2 = lax.broadcasted_iota(jnp.int32, (chunk, 2 * chunk), 0)
    cc2 = lax.broadcasted_iota(jnp.int32, (chunk, 2 * chunk), 1) % chunk
    tril2 = jnp.where(cc2 <= rr2, 1.0, 0.0).astype(BF16)

    def gates(c0, h):
        hs = slice(h * d, (h + 1) * d)
        rows = pl.ds(c0, chunk)
        lb = lb_all[:, hs]
        t1 = (0.5 * (1.0 - lb)) * jnp.tanh(f_ref[0, rows, hs].astype(F32))
        log_f = jnp.log(0.5 * (1.0 + lb) + t1)
        k_in = 0.5 * (1.0 - lb) - t1
        hq = q_ref[0, rows, hs].astype(F32)
        qs = hq + hq * jnp.tanh(hq)
        return qs, k_in, log_f

    def finish(c0, h, out):
        hs = slice(h * d, (h + 1) * d)
        rows = pl.ds(c0, chunk)
        hg = g_ref[0, rows, hs].astype(F32)
        gate = hg + hg * jnp.tanh(hg)
        o_ref[0, rows, hs] = (_rms(out, nw) * gate).astype(o_ref.dtype)

    def fixup(c0, u):
        ones8 = jnp.ones((8, d), BF16)
        scores = cum_ref.at[cpi * heads]
        for h in range(heads):
            hs = slice(h * d, (h + 1) * d)
            qs, k_in, _unused = gates(c0, h)
            cum = cum_ref[u * heads + h]
            v = i_ref[0, pl.ds(c0, chunk), hs]

            def row(t, _c):
                sel = rr[:, :1] == t
                q_t = jnp.sum(jnp.where(sel, qs, 0.0), axis=0, keepdims=True)
                cum_t = jnp.sum(jnp.where(sel, cum, 0.0), axis=0, keepdims=True)
                p = q_t * k_in * jnp.exp(jnp.minimum(cum_t - cum, 0.0))
                srow = lax.dot_general(ones8, p.astype(BF16), _NT,
                                       preferred_element_type=F32)
                scores[pl.ds(t, 1), :] = jnp.where(cc[:1, :] <= t, srow[:1, :], 0.0)
                return _c

            lax.fori_loop(0, chunk, row, 0)
            qg = (qs * jnp.exp(cum)).astype(BF16)
            out = (jnp.dot(scores[...].astype(BF16), v, preferred_element_type=F32)
                   + lax.dot_general(qg, st_ref[u, h].astype(BF16), _NT,
                                     preferred_element_type=F32))
            finish(c0, h, out)

    hr = range(heads)
    hsl = [slice(h * d, (h + 1) * d) for h in hr]
    c0 = [u * chunk for u in range(cpi)]
    uh = [(u, h) for u in range(cpi) for h in hr]
    g = {k: gates(c0[k[0]], k[1]) for k in uh}
    cum = {}
    for u, h in uh:
        hi, lo = _split_bf16(g[u, h][2])
        cum[u, h] = jnp.dot(tril2, jnp.concatenate([hi, lo], axis=0),
                            preferred_element_type=F32)
        cum_ref[u * heads + h] = cum[u, h]
    last = {k: cum[k][chunk - 1:chunk, :] for k in uh}
    qg = {k: (g[k][0] * jnp.exp(cum[k])).astype(BF16) for k in uh}
    kd = {k: g[k][1] * jnp.exp(last[k] - cum[k]) for k in uh}
    kgt = {k: (kd[k] * jnp.exp(jnp.minimum(-last[k], HG_SAFE_LOG))).T.astype(BF16)
           for k in uh}
    v = {(u, h): i_ref[0, pl.ds(c0[u], chunk), hsl[h]] for u, h in uh}
    a = {k: jnp.dot(qg[k], kgt[k], preferred_element_type=F32) for k in uh}
    st = [st_ref[0, h] for h in hr]
    inter = {}
    for u in range(cpi):
        for h in hr:
            inter[u, h] = jnp.dot(qg[u, h], st[h].T.astype(BF16),
                                  preferred_element_type=F32)
        for h in hr:
            vt = v[u, h].astype(F32).T.astype(BF16)
            st[h] = (st[h] * jnp.exp(last[u, h])
                     + jnp.dot(vt, kd[u, h].astype(BF16), preferred_element_type=F32))
            st_ref[u + 1, h] = st[h]
    for u, h in uh:
        am = jnp.where(incl, a[u, h], 0.0).astype(BF16)
        finish(c0[u], h, jnp.dot(am, v[u, h], preferred_element_type=F32) + inter[u, h])
    worst = [functools.reduce(jnp.minimum, [jnp.min(last[u, h]) for h in hr])
             for u in range(cpi)]

    yield

    for u in range(cpi):
        pl.when(worst[u] < -HG_SAFE_LOG)(functools.partial(fixup, c0[u], u))

    st_ref[0] = st_ref[cpi]


def _mixers_kernel(*refs, heads, blk, nprev, ncast, k_col, v_col, chunk, layer):
    nkv = nprev + 1
    it = iter(refs)
    take = lambda n: [next(it) for _ in range(n)]
    (sb_q,), sb_k, sb_v, (proj_hbm, sb_nw), cast_in = take(1), take(nkv), take(nkv), take(2), take(ncast)
    hg_in = take(6)
    (o_a,), cast_out, (o_b,) = take(1), take(ncast), take(1)
    acc_ref, carry_ref, kbuf, vbuf, sem, st_ref, cum_ref = take(7)

    sb = _sb_body(sb_q, sb_k, sb_v, proj_hbm, sb_nw, cast_in, o_a, cast_out,
                  acc_ref, carry_ref, kbuf, vbuf, sem,
                  heads=heads, blk=blk, nprev=nprev, k_col=k_col, v_col=v_col)
    hg = _hg_body(*hg_in, o_b, st_ref, cum_ref, heads=heads, chunk=chunk, layer=layer)
    next(sb)
    next(hg)
    for tail in (sb, hg):
        for _ in tail:
            pass


def _mixers(proj3, sb_norm_w, lb_logits, hg_norm_w, cast_weights, *, width, blk, nprev, layer):
    b, s, _ = proj3.shape
    heads = width // HEAD_DIM
    chunk = HEAD_DIM
    assert blk == HG_CHUNKS_PER_STEP * chunk
    nq = s // blk
    steps = b * nq
    for w in cast_weights:
        assert w.shape[0] % (steps * 16) == 0, "a bf16 slab needs a multiple of 16 rows"
    cast_specs = [pl.BlockSpec((w.shape[0] // steps, w.shape[1]), lambda bb, i: (bb * nq + i, 0))
                  for w in cast_weights]
    kern = functools.partial(_mixers_kernel, heads=heads, blk=blk, nprev=nprev,
                             ncast=len(cast_weights), k_col=width, v_col=2 * width,
                             chunk=chunk, layer=layer)

    def back(k, col):
        return pl.BlockSpec((1, blk, width), lambda bb, i: (bb, jnp.maximum(i - k, 0), col))

    head_w = pl.BlockSpec((1, HEAD_DIM), lambda bb, i: (0, 0))
    return pl.pallas_call(
        kern,
        grid=(b, nq),
        in_specs=[
            back(0, 0),
            *[back(k, 1) for k in range(nprev + 1)],
            *[back(k, 2) for k in range(nprev + 1)],
            pl.BlockSpec(memory_space=pl.ANY),
            head_w,
            *cast_specs,
            back(0, 3), back(0, 4), back(0, 5), back(0, 6),
            pl.BlockSpec(lb_logits.shape, lambda bb, i: (0, 0)),
            head_w,
        ],
        out_specs=[back(0, 0), *cast_specs, back(0, 0)],
        out_shape=[jax.ShapeDtypeStruct((b, s, width), BF16),
                   *[jax.ShapeDtypeStruct(w.shape, BF16) for w in cast_weights],
                   jax.ShapeDtypeStruct((b, s, width), BF16)],
        scratch_shapes=[pltpu.VMEM((heads, blk, HEAD_DIM), F32),
                        pltpu.VMEM((heads * blk, LANES), F32),
                        pltpu.VMEM((blk, width), BF16),
                        pltpu.VMEM((blk, width), BF16),
                        pltpu.SemaphoreType.DMA((2,)),
                        pltpu.VMEM((HG_CHUNKS_PER_STEP + 1, heads, HEAD_DIM, HEAD_DIM), F32),
                        pltpu.VMEM((HG_CHUNKS_PER_STEP * heads + 1, chunk, HEAD_DIM), F32)],
        compiler_params=pltpu.CompilerParams(
            dimension_semantics=("parallel", "arbitrary"), vmem_limit_bytes=VMEM_LIMIT_BYTES),
        name="mixers",
    )(*([proj3] * (4 + 2 * nprev)), sb_norm_w.reshape(1, HEAD_DIM), *cast_weights,
      proj3, proj3, proj3, proj3, lb_logits, hg_norm_w.reshape(1, HEAD_DIM))


def _out_proj_kernel(oa_ref, ob_ref, wa_ref, wb_ref, x_ref, nw_ref, h_ref, m_ref):
    acc = (jnp.dot(oa_ref[...], wa_ref[...], preferred_element_type=F32)
           + jnp.dot(ob_ref[...], wb_ref[...], preferred_element_type=F32))
    h = x_ref[...] + acc
    h_ref[...] = h
    m_ref[...] = _rms(h, nw_ref[...]).astype(m_ref.dtype)


def _out_proj(oa, ob, w_out, x2, norm_w, *, tm):
    t, d = x2.shape
    wa = oa.shape[1]
    wb = ob.shape[1]
    assert wa == wb
    return pl.pallas_call(
        _out_proj_kernel,
        grid=(t // tm,),
        in_specs=[
            pl.BlockSpec((tm, wa), lambda i: (i, 0)),
            pl.BlockSpec((tm, wb), lambda i: (i, 0)),
            pl.BlockSpec((wa, d), lambda i: (0, 0)),
            pl.BlockSpec((wb, d), lambda i: (1, 0)),
            pl.BlockSpec((tm, d), lambda i: (i, 0)),
            pl.BlockSpec((1, d), lambda i: (0, 0)),
        ],
        out_specs=[pl.BlockSpec((tm, d), lambda i: (i, 0)),
                   pl.BlockSpec((tm, d), lambda i: (i, 0))],
        out_shape=[jax.ShapeDtypeStruct((t, d), F32), jax.ShapeDtypeStruct((t, d), BF16)],
        compiler_params=pltpu.CompilerParams(
            dimension_semantics=("parallel",), vmem_limit_bytes=VMEM_LIMIT_BYTES),
        name="out_proj",
    )(oa, ob, w_out, w_out, x2, norm_w.reshape(1, d))


def _mlp_kernel(m_ref, wu_ref, wd_ref, h_hbm, nw_ref, y_hbm, acc_ref, sem):
    i = pl.program_id(0)
    j = pl.program_id(1)
    ni = pl.num_programs(0)
    nj = pl.num_programs(1)
    tm = acc_ref.shape[1]
    s = i % 2

    def h_copy(blk, slot):
        return pltpu.make_async_copy(h_hbm.at[pl.ds(blk * tm, tm)], acc_ref.at[slot],
                                     sem.at[0, slot])

    def y_copy(blk, slot):
        return pltpu.make_async_copy(acc_ref.at[slot], y_hbm.at[pl.ds(blk * tm, tm)],
                                     sem.at[1, slot])

    @pl.when(jnp.logical_and(i == 0, j == 0))
    def _():
        h_copy(0, 0).start()

    @pl.when(j == 0)
    def _():
        h_copy(i, s).wait()

    @pl.when(j == 1)
    def _():
        @pl.when(i >= 1)
        def _():
            y_copy(i - 1, 1 - s).wait()

        @pl.when(i + 1 < ni)
        def _():
            h_copy(i + 1, 1 - s).start()

    r = jnp.maximum(jnp.dot(m_ref[...], wu_ref[...], preferred_element_type=F32), 0.0)
    acc_ref[s] += jnp.dot((r * r).astype(BF16), wd_ref[...], preferred_element_type=F32)

    @pl.when(j == nj - 1)
    def _():
        acc_ref[s] = _rms(acc_ref[s], nw_ref[...])
        y_copy(i, s).start()

        @pl.when(i == ni - 1)
        def _():
            y_copy(i, s).wait()


def _mlp(m, w_up, w_down, h, norm_w, *, tm, tf):
    t, d = m.shape
    f = w_up.shape[1]
    assert f // tf >= 2, "the slot hand-over happens at the second inner step"
    return pl.pallas_call(
        _mlp_kernel,
        grid=(t // tm, f // tf),
        in_specs=[pl.BlockSpec((tm, d), lambda i, j: (i, 0)),
                  pl.BlockSpec((d, tf), lambda i, j: (0, j)),
                  pl.BlockSpec((tf, d), lambda i, j: (j, 0)),
                  pl.BlockSpec(memory_space=pl.ANY),
                  pl.BlockSpec((1, d), lambda i, j: (0, 0))],
        out_specs=pl.BlockSpec(memory_space=pl.ANY),
        out_shape=jax.ShapeDtypeStruct((t, d), F32),
        scratch_shapes=[pltpu.VMEM((2, tm, d), F32), pltpu.SemaphoreType.DMA((2, 2))],
        compiler_params=pltpu.CompilerParams(
            dimension_semantics=("arbitrary", "arbitrary"), vmem_limit_bytes=VMEM_LIMIT_BYTES),
        name="mlp",
    )(m, w_up, w_down, h, norm_w.reshape(1, d))


def _block(x, attn_norm_w, w_in, lb_logits, sb_norm_w, hg_norm_w, w_out,
           mlp_norm_w, w_up, w_down, final_norm_w, *, tiles):
    b, s, d = x.shape
    depth = w_in.shape[0]
    sb_width = d // 2
    hg_width = d // 2
    x2 = x.reshape(b * s, d)
    h = x2
    for layer in range(depth):
        proj = _in_proj(h, attn_norm_w[layer], w_in[layer],
                        group_scales=(HEAD_DIM ** -0.5 * LOG2_E, 1.0, 1.0, 0.5, 0.5, 1.0, 0.5),
                        tm=tiles["proj_tm"], tn=tiles["proj_tn"])
        proj3 = proj.reshape(b, s, -1)
        o_a, w_out_bf, w_up_bf, w_down_bf, o_b = _mixers(
            proj3, sb_norm_w[layer], lb_logits, hg_norm_w[layer],
            (w_out[layer], w_up[layer], w_down[layer]),
            width=sb_width, blk=tiles["mix_blk"], nprev=tiles["sb_prev"], layer=layer)
        h, m = _out_proj(o_a.reshape(b * s, sb_width), o_b.reshape(b * s, hg_width),
                         w_out_bf, h, mlp_norm_w[layer], tm=tiles["out_tm"])
        assert layer == depth - 1, "only the single-layer block is fused end to end"
        h = _mlp(m, w_up_bf, w_down_bf, h, final_norm_w,
                 tm=tiles["mlp_tm"], tf=tiles["mlp_tf"])
    return h.reshape(b, s, d)


_TILES = dict(proj_tm=2048, proj_tn=512, mix_blk=256, sb_prev=1,
              out_tm=512, mlp_tm=1024, mlp_tf=1024)


def kernel(x, attn_norm_w, w_in, lb_logits, sb_norm_w, hg_norm_w, w_out,
           mlp_norm_w, w_up, w_down, final_norm_w):
    return _block(x, attn_norm_w, w_in, lb_logits, sb_norm_w, hg_norm_w, w_out,
                  mlp_norm_w, w_up, w_down, final_norm_w, tiles=_TILES)
```
